```python
import math
import jax, jax.numpy as jnp
from jax import lax
import numpy as np

D_MODEL = 1024
BATCH = 8
SEQ = 4096
DEPTH = 4

CHUNK = 64
N_MIXERS = 4
EPS = 1e-6
GLA_HEADS = 4
GLA_DK = (D_MODEL // 2) // GLA_HEADS
GLA_DV = D_MODEL // GLA_HEADS
GLA_GATE_RANK = 16
GLA_TAU = 16.0
GLA_IN = 2 * GLA_HEADS * GLA_DK + 2 * GLA_HEADS * GLA_DV + GLA_GATE_RANK
POOL_WINDOWS = (2, 4, 8, 16)
POOL_GROUPS = len(POOL_WINDOWS)
POOL_GW = D_MODEL // POOL_GROUPS
SSD_DINNER = 2 * D_MODEL
SSD_HEADDIM = 64
SSD_HEADS = SSD_DINNER // SSD_HEADDIM
SSD_GROUPS = 4
SSD_HPG = SSD_HEADS // SSD_GROUPS
SSD_STATE = 128
SSD_CONV = 4
SSD_CONV_DIM = SSD_DINNER + 2 * SSD_GROUPS * SSD_STATE
SSD_IN = SSD_DINNER + SSD_CONV_DIM + SSD_HEADS
SB_HEADS = 16
SB_HEADDIM = D_MODEL // SB_HEADS
SB_QBLOCK = 128
FFN_DIM = 2816
FFN_CONV = 3

kernel_name = 'hybrid_chunk_causal_encoder'


def rms_normalize(x):
    xf = x.astype(jnp.float32)
    y = xf * lax.rsqrt(jnp.mean(xf * xf, axis=-1, keepdims=True) + EPS)
    return y.astype(x.dtype)


def rmsnorm(x, w):
    return rms_normalize(x) * w


def causal_dwconv(u, w, b):
    K = w.shape[0]
    L = u.shape[1]
    up = jnp.pad(u, ((0, 0), (K - 1, 0), (0, 0)))
    out = up[:, K - 1:K - 1 + L] * w[K - 1] + b
    for k in range(K - 1):
        out = out + up[:, k:k + L] * w[k]
    return out


def to_chunks(a):
    b_, l_ = a.shape[:2]
    return jnp.moveaxis(a.reshape(b_, l_ // CHUNK, CHUNK, *a.shape[2:]), 1, 0)


def from_chunks(a):
    a = jnp.moveaxis(a, 0, 1)
    return a.reshape(a.shape[0], a.shape[1] * a.shape[2], *a.shape[3:])


def gla_mixer(h, w_in, w_gate2, b_gate, norm_w, w_out):
    b_, l_, _ = h.shape
    hk, hv = GLA_HEADS * GLA_DK, GLA_HEADS * GLA_DV
    proj = h @ w_in
    q, k, v, r, glr = jnp.split(proj, [hk, 2 * hk, 2 * hk + hv, 2 * hk + 2 * hv], axis=-1)
    q = q.reshape(b_, l_, GLA_HEADS, GLA_DK).astype(jnp.float32) * (GLA_DK ** -0.5)
    k = k.reshape(b_, l_, GLA_HEADS, GLA_DK).astype(jnp.float32)
    v = v.reshape(b_, l_, GLA_HEADS, GLA_DV).astype(jnp.float32)
    log_a = jax.nn.log_sigmoid((glr @ w_gate2 + b_gate).astype(jnp.float32)) / GLA_TAU
    log_a = log_a.reshape(b_, l_, GLA_HEADS, GLA_DK)
    qc, kc, vc, lac = to_chunks(q), to_chunks(k), to_chunks(v), to_chunks(log_a)
    g = jnp.cumsum(lac, axis=2)
    g_end = g[:, :, -1]
    k_dec = kc * jnp.exp(g_end[:, :, None] - g)

    def body(state, inp):
        q_c, kd_c, v_c, ge_c = inp
        state = state * jnp.exp(ge_c)[..., None] + jnp.einsum('bshk,bshv->bhkv', kd_c, v_c)
        o = jnp.einsum('bqhk,bhkv->bqhv', q_c, state)
        return state, o

    s0 = jnp.zeros((b_, GLA_HEADS, GLA_DK, GLA_DV), jnp.float32)
    _, o = lax.scan(body, s0, (qc, k_dec, vc, g_end))
    o = rms_normalize(from_chunks(o)).reshape(b_, l_, hv) * norm_w
    o = o.astype(h.dtype) * jax.nn.silu(r)
    return o @ w_out


def pool_mixer(h, w_grp, b_grp, scale):
    b_, l_, d_ = h.shape
    hf = h.astype(jnp.float32)
    cs = jnp.concatenate([jnp.zeros((b_, 1, d_), jnp.float32), jnp.cumsum(hf, axis=1)], axis=1)
    t1 = jnp.arange(1, l_ + 1, dtype=jnp.float32)
    outs = []
    for gi, win in enumerate(POOL_WINDOWS):
        sl = slice(gi * POOL_GW, (gi + 1) * POOL_GW)
        c = cs[..., sl]
        lower = jnp.concatenate([jnp.zeros((b_, win - 1, POOL_GW), jnp.float32), c[:, :l_ + 1 - win]], axis=1)
        mean = (c[:, 1:] - lower) / jnp.minimum(t1, float(win))[None, :, None]
        dlt = (mean - hf[..., sl]).astype(h.dtype)
        outs.append(dlt @ w_grp[gi] + b_grp[gi])
    return jnp.concatenate(outs, axis=-1) * scale


def ssd_mixer(h, w_in, conv_w, conv_b, dt_bias, a_log, d_skip, norm_w, w_out):
    b_, l_, _ = h.shape
    G, HG, P, N = SSD_GROUPS, SSD_HPG, SSD_HEADDIM, SSD_STATE
    proj = h @ w_in
    z, xbc, dt = jnp.split(proj, [SSD_DINNER, SSD_DINNER + SSD_CONV_DIM], axis=-1)
    xbc = jax.nn.silu(causal_dwconv(xbc, conv_w, conv_b))
    xs, bm, cm = jnp.split(xbc, [SSD_DINNER, SSD_DINNER + G * N], axis=-1)
    xs = xs.reshape(b_, l_, G, HG, P)
    bm = bm.reshape(b_, l_, G, N)
    cm = cm.reshape(b_, l_, G, N)
    dt = jax.nn.softplus((dt + dt_bias).astype(jnp.float32)).reshape(b_, l_, G, HG)
    a_h = -jnp.exp(a_log.astype(jnp.float32)).reshape(G, HG)
    causal = jnp.tril(jnp.ones((CHUNK, CHUNK), bool))

    def body(state, inp):
        x_c, b_c, c_c, dt_c = inp
        acs = jnp.cumsum(dt_c * a_h, axis=1)
        seg = acs[:, :, None] - acs[:, None, :]
        lmat = jnp.exp(jnp.where(causal[None, :, :, None, None], seg, -jnp.inf))
        xdt = x_c * dt_c[..., None]
        cb = jnp.einsum('btgn,bsgn->btsg', c_c, b_c)
        y = jnp.einsum('btsg,btsgh,bsghp->btghp', cb, lmat, xdt)
        y = y + jnp.einsum('btgn,bghpn,btgh->btghp', c_c, state, jnp.exp(acs))
        decay_end = jnp.exp(acs[:, -1:] - acs)
        state = state * jnp.exp(acs[:, -1])[..., None, None] + jnp.einsum('bsgn,bsgh,bsghp->bghpn', b_c, decay_end, xdt)
        return state, y

    s0 = jnp.zeros((b_, G, HG, P, N), jnp.float32)
    _, ys = lax.scan(body, s0, (to_chunks(xs), to_chunks(bm), to_chunks(cm), to_chunks(dt)))
    y = from_chunks(ys) + d_skip.reshape(G, HG)[..., None] * xs
    y = y.reshape(b_, l_, SSD_DINNER) * jax.nn.silu(z)
    y = rms_normalize(y.reshape(b_, l_, G, SSD_DINNER // G)).reshape(b_, l_, SSD_DINNER) * norm_w
    return y.astype(h.dtype) @ w_out


def sb_mixer(h, w_qkv, w_out):
    b_, l_, _ = h.shape
    q, k, v = jnp.split(h @ w_qkv, 3, axis=-1)
    q = q.reshape(b_, l_, SB_HEADS, SB_HEADDIM).transpose(0, 2, 1, 3) * (SB_HEADDIM ** -0.5)
    k = k.reshape(b_, l_, SB_HEADS, SB_HEADDIM).transpose(0, 2, 1, 3)
    v = v.reshape(b_, l_, SB_HEADS, SB_HEADDIM).transpose(0, 2, 1, 3)
    nb = l_ // SB_QBLOCK
    qb = jnp.moveaxis(q.reshape(b_, SB_HEADS, nb, SB_QBLOCK, SB_HEADDIM), 2, 0)
    starts = jnp.arange(nb, dtype=jnp.int32) * SB_QBLOCK
    key_pos = jnp.arange(l_, dtype=jnp.int32)

    def block(inp):
        q_i, start = inp
        zt = jnp.einsum('bhqd,bhkd->bhqk', q_i, k).astype(jnp.float32)
        qpos = start + jnp.arange(SB_QBLOCK, dtype=jnp.int32)
        mask = key_pos[None, :] < qpos[:, None]
        log_1mb = jnp.where(mask, jax.nn.log_sigmoid(-zt), 0.0)
        rc = lax.cumsum(log_1mb, axis=3, reverse=True)
        surv = jnp.concatenate([rc[..., 1:], jnp.zeros_like(rc[..., :1])], axis=-1)
        att = jnp.where(mask, jnp.exp(jax.nn.log_sigmoid(zt) + surv), 0.0)
        return jnp.einsum('bhqk,bhkd->bhqd', att.astype(v.dtype), v)

    o = lax.map(block, (qb, starts))
    o = jnp.moveaxis(o, 0, 2).reshape(b_, SB_HEADS, l_, SB_HEADDIM)
    o = o.transpose(0, 2, 1, 3).reshape(b_, l_, D_MODEL)
    return o @ w_out


def conv_ffn(h, w_up, conv_w, conv_b, w_down):
    u = causal_dwconv(h @ w_up, conv_w, conv_b)
    g, val = jnp.split(u, 2, axis=-1)
    return (jax.nn.silu(g) * val) @ w_down


def setup_inputs(seed: int = 0) -> dict:
    key = jax.random.key(seed)
    ks = list(jax.random.split(key, 40))
    f32 = jnp.float32

    def nrm(i, shape, scale):
        return jax.random.normal(ks[i], shape, f32) * scale

    n_a = len(range(0, DEPTH, N_MIXERS))
    n_b = len(range(1, DEPTH, N_MIXERS))
    n_c = len(range(2, DEPTH, N_MIXERS))
    n_d = len(range(3, DEPTH, N_MIXERS))
    dt0 = jnp.exp(jax.random.uniform(ks[20], (n_c, SSD_HEADS), f32) * (math.log(0.1) - math.log(0.001)) + math.log(0.001))
    return {
        'x': nrm(0, (BATCH, SEQ, D_MODEL), 1.0),
        'mix_norm_w': 1.0 + nrm(1, (DEPTH, D_MODEL), 0.02),
        'ffn_norm_w': 1.0 + nrm(2, (DEPTH, D_MODEL), 0.02),
        'final_norm_w': 1.0 + nrm(3, (D_MODEL,), 0.02),
        'gla_w_in': nrm(4, (n_a, D_MODEL, GLA_IN), D_MODEL ** -0.5),
        'gla_w_gate2': nrm(5, (n_a, GLA_GATE_RANK, GLA_HEADS * GLA_DK), GLA_GATE_RANK ** -0.5),
        'gla_b_gate': nrm(6, (n_a, GLA_HEADS * GLA_DK), 0.1),
        'gla_norm_w': 1.0 + nrm(7, (n_a, GLA_HEADS * GLA_DV), 0.02),
        'gla_w_out': nrm(8, (n_a, GLA_HEADS * GLA_DV, D_MODEL), (GLA_HEADS * GLA_DV) ** -0.5),
        'pool_w': nrm(9, (n_b, POOL_GROUPS, POOL_GW, POOL_GW), POOL_GW ** -0.5),
        'pool_b': nrm(10, (n_b, POOL_GROUPS, POOL_GW), 0.02),
        'pool_scale': 1.0 + nrm(11, (n_b, D_MODEL), 0.02),
        'ssd_w_in': nrm(12, (n_c, D_MODEL, SSD_IN), D_MODEL ** -0.5),
        'ssd_conv_w': nrm(13, (n_c, SSD_CONV, SSD_CONV_DIM), SSD_CONV ** -0.5),
        'ssd_conv_b': nrm(14, (n_c, SSD_CONV_DIM), 0.02),
        'ssd_dt_bias': dt0 + jnp.log(-jnp.expm1(-dt0)),
        'ssd_a_log': jnp.log(jax.random.uniform(ks[15], (n_c, SSD_HEADS), f32, 1.0, 16.0)),
        'ssd_d': 1.0 + nrm(16, (n_c, SSD_HEADS), 0.02),
        'ssd_norm_w': 1.0 + nrm(17, (n_c, SSD_DINNER), 0.02),
        'ssd_w_out': nrm(18, (n_c, SSD_DINNER, D_MODEL), SSD_DINNER ** -0.5),
        'sb_w_qkv': nrm(19, (n_d, D_MODEL, 3 * D_MODEL), D_MODEL ** -0.5),
        'sb_w_out': nrm(21, (n_d, D_MODEL, D_MODEL), D_MODEL ** -0.5),
        'ffn_w_up': nrm(22, (DEPTH, D_MODEL, 2 * FFN_DIM), D_MODEL ** -0.5),
        'ffn_conv_w': nrm(23, (DEPTH, FFN_CONV, 2 * FFN_DIM), FFN_CONV ** -0.5),
        'ffn_conv_b': nrm(24, (DEPTH, 2 * FFN_DIM), 0.02),
        'ffn_w_down': nrm(25, (DEPTH, FFN_DIM, D_MODEL), FFN_DIM ** -0.5),
    }


def reference(x, mix_norm_w, ffn_norm_w, final_norm_w,
              gla_w_in, gla_w_gate2, gla_b_gate, gla_norm_w, gla_w_out,
              pool_w, pool_b, pool_scale,
              ssd_w_in, ssd_conv_w, ssd_conv_b, ssd_dt_bias, ssd_a_log, ssd_d, ssd_norm_w, ssd_w_out,
              sb_w_qkv, sb_w_out,
              ffn_w_up, ffn_conv_w, ffn_conv_b, ffn_w_down):
    for i in range(DEPTH):
        m, j = i % N_MIXERS, i // N_MIXERS
        h = rmsnorm(x, mix_norm_w[i])
        if m == 0:
            y = gla_mixer(h, gla_w_in[j], gla_w_gate2[j], gla_b_gate[j], gla_norm_w[j], gla_w_out[j])
        elif m == 1:
            y = pool_mixer(h, pool_w[j], pool_b[j], pool_scale[j])
        elif m == 2:
            y = ssd_mixer(h, ssd_w_in[j], ssd_conv_w[j], ssd_conv_b[j], ssd_dt_bias[j], ssd_a_log[j],
                          ssd_d[j], ssd_norm_w[j], ssd_w_out[j])
        else:
            y = sb_mixer(h, sb_w_qkv[j], sb_w_out[j])
        x = x + y.astype(x.dtype)
        h = rmsnorm(x, ffn_norm_w[i])
        x = x + conv_ffn(h, ffn_w_up[i], ffn_conv_w[i], ffn_conv_b[i], ffn_w_down[i]).astype(x.dtype)
    return rmsnorm(x, final_norm_w)
```

```python
import functools

import jax
import jax.numpy as jnp
from jax import lax
from jax.experimental import pallas as pl
from jax.experimental.pallas import tpu as pltpu

F32 = jnp.float32
BF16 = jnp.bfloat16

EPS = 1e-6
D_MODEL = 1024
HALO = 16
VMEM_LIMIT = 56 * 1024 * 1024

GLA_HEADS, GLA_DK, GLA_DV, GLA_RANK, GLA_TAU, GLA_CHUNK = 4, 128, 256, 16, 16.0, 64
GLA_HK, GLA_HV = GLA_HEADS * GLA_DK, GLA_HEADS * GLA_DV
POOL_WINDOWS, POOL_GW = (2, 4, 8, 16), 256
SSD_DINNER, SSD_P, SSD_HEADS, SSD_GROUPS, SSD_HPG, SSD_N, SSD_CONV = 2048, 64, 32, 4, 8, 128, 4
SSD_GN = SSD_GROUPS * SSD_N
SSD_Q = 128
SB_HEADS, SB_DH, SB_T = 16, 64, 256
FFN_DIM, FFN_TILE, FFN_CONV = 2816, 256, 3
FFN_NT = FFN_DIM // FFN_TILE


def _params(*sem):
    return pltpu.CompilerParams(dimension_semantics=sem, vmem_limit_bytes=VMEM_LIMIT)


def _vmem():
    return pl.BlockSpec(memory_space=pltpu.VMEM)


def _rms(x):
    return x * lax.rsqrt(jnp.mean(x * x, axis=-1, keepdims=True) + EPS)


def _softplus(x):
    return jnp.maximum(x, 0.0) + jnp.log(1.0 + jnp.exp(-jnp.abs(x)))


def _sigmoid(x):
    return 1.0 / (1.0 + jnp.exp(-x))


def _dot(a, b):
    return jnp.dot(a, b, preferred_element_type=F32)


def _dot_nt(a, b):
    return lax.dot_general(a, b, (((1,), (1,)), ((), ())), preferred_element_type=F32)


def _dot_tn(a, b):
    return lax.dot_general(a, b, (((0,), (0,)), ((), ())), preferred_element_type=F32)


def _split3(x):
    hi = x.astype(BF16)
    r = x - hi.astype(F32)
    mid = r.astype(BF16)
    lo = (r - mid.astype(F32)).astype(BF16)
    return hi, mid, lo


def _exact_left(m01, x):
    hi, mid, lo = _split3(x)
    return _dot(m01, lo) + _dot(m01, mid) + _dot(m01, hi)


def _exact_right(x, m01):
    hi, mid, lo = _split3(x)
    return _dot(lo, m01) + _dot(mid, m01) + _dot(hi, m01)


def _tri(n, lower):
    r = lax.broadcasted_iota(jnp.int32, (n, n), 0)
    c = lax.broadcasted_iota(jnp.int32, (n, n), 1)
    return jnp.where((r >= c) if lower else (r <= c), 1.0, 0.0).astype(BF16)


def _prev_rows_spec(tm, steps_per_seq):
    del steps_per_seq
    return pl.BlockSpec((HALO, D_MODEL), lambda i: (jnp.maximum(i * (tm // HALO) - 1, 0), 0))


def _normed_with_halo(x_ref, xp_ref, nw_ref, hs_ref, steps_per_seq):
    first = (pl.program_id(0) % steps_per_seq) == 0
    nw = nw_ref[...]
    hp = _rms(xp_ref[...]) * nw
    hs_ref[0:HALO, :] = jnp.where(first, 0.0, hp).astype(hs_ref.dtype)
    hs_ref[HALO:, :] = (_rms(x_ref[...]) * nw).astype(hs_ref.dtype)


def _ffn_kernel(x_ref, xp_ref, nw_ref, wg_ref, wv_ref, cg_ref, cv_ref, wd_ref, fw_ref, o_ref,
                hs_ref, acc_ref, *, steps_per_seq, final_norm):
    tm = x_ref.shape[0]
    _normed_with_halo(x_ref, xp_ref, nw_ref, hs_ref, steps_per_seq)
    acc_ref[...] = jnp.zeros_like(acc_ref)

    def conv(u, c):
        out = u[HALO:] * c[2:3] + c[3:4]
        out = out + pltpu.roll(u, 1, 0)[HALO:] * c[1:2]
        out = out + pltpu.roll(u, 2, 0)[HALO:] * c[0:1]
        return out

    def body(j, carry):
        hs = hs_ref[...]
        g = conv(_dot(hs, wg_ref[j]), cg_ref[j])
        v = conv(_dot(hs, wv_ref[j]), cv_ref[j])
        a = (g * _sigmoid(g) * v).astype(BF16)
        acc_ref[...] += _dot(a, wd_ref[j])
        return carry

    lax.fori_loop(0, FFN_NT, body, 0)
    y = x_ref[...] + acc_ref[...]
    if final_norm:
        y = _rms(y) * fw_ref[...]
    o_ref[...] = y
    del tm


def _ffn(x, nw, w_up, conv_w, conv_b, w_down, final_w, seq_len, final_norm, tm=512):
    t = x.shape[0]
    steps_per_seq = seq_len // tm

    def tiles(w):
        return w.reshape(D_MODEL, FFN_NT, FFN_TILE).transpose(1, 0, 2).astype(BF16)

    def conv_tiles(cw, cb):
        c = jnp.concatenate([cw, cb[None, :], jnp.zeros((4, FFN_DIM), F32)], axis=0)
        return c.reshape(8, FFN_NT, FFN_TILE).transpose(1, 0, 2)

    wg, wv = tiles(w_up[:, :FFN_DIM]), tiles(w_up[:, FFN_DIM:])
    cg = conv_tiles(conv_w[:, :FFN_DIM], conv_b[:FFN_DIM])
    cv = conv_tiles(conv_w[:, FFN_DIM:], conv_b[FFN_DIM:])
    wd = w_down.reshape(FFN_NT, FFN_TILE, D_MODEL).astype(BF16)
    row = pl.BlockSpec((tm, D_MODEL), lambda i: (i, 0))
    return pl.pallas_call(
        functools.partial(_ffn_kernel, steps_per_seq=steps_per_seq, final_norm=final_norm),
        grid=(t // tm,),
        in_specs=[row, _prev_rows_spec(tm, steps_per_seq), _vmem(), _vmem(), _vmem(), _vmem(), _vmem(),
                  _vmem(), _vmem()],
        out_specs=row,
        out_shape=jax.ShapeDtypeStruct((t, D_MODEL), F32),
        scratch_shapes=[pltpu.VMEM((tm + HALO, D_MODEL), BF16), pltpu.VMEM((tm, D_MODEL), F32)],
        compiler_params=_params("parallel"),
        name="conv_ffn",
    )(x, x, nw.reshape(1, D_MODEL), wg, wv, cg, cv, wd, final_w.reshape(1, D_MODEL))


def _gla_kernel(x_ref, nw_ref, wq_ref, wk_ref, wv_ref, wr_ref, wg1_ref, wg2_ref, bg_ref, gnw_ref, wo_ref,
                o_ref, st_ref, q_s, k_s, v_s, la_s, o_s):
    tm = x_ref.shape[0]

    @pl.when(pl.program_id(1) == 0)
    def _():
        st_ref[...] = jnp.zeros_like(st_ref)

    x = x_ref[...]
    h = (_rms(x) * nw_ref[...]).astype(BF16)
    q_s[...] = _dot(h, wq_ref[...]).astype(BF16)
    k_s[...] = _dot(h, wk_ref[...])
    v_s[...] = _dot(h, wv_ref[...]).astype(BF16)
    glr = _dot(h, wg1_ref[...]).astype(BF16)
    gate = _dot(glr, wg2_ref[...]) + bg_ref[...]
    la_s[...] = -_softplus(-gate) * (1.0 / GLA_TAU)

    tril = _tri(GLA_CHUNK, lower=True)
    scale = GLA_DK ** -0.5

    def chunk(c, carry):
        rows = pl.ds(pl.multiple_of(c * GLA_CHUNK, GLA_CHUNK), GLA_CHUNK)
        g = _exact_left(tril, la_s[rows, :])
        g_end = g[GLA_CHUNK - 1:GLA_CHUNK, :]
        kd = (k_s[rows, :] * jnp.exp(g_end - g)).astype(BF16)
        eg = jnp.exp(g_end)
        qc = q_s[rows, :]
        vc = v_s[rows, :]
        for hd in range(GLA_HEADS):
            ks = slice(hd * GLA_DK, (hd + 1) * GLA_DK)
            vs = slice(hd * GLA_DV, (hd + 1) * GLA_DV)
            st = st_ref[hd] * eg[:, ks] + _dot_tn(vc[:, vs], kd[:, ks])
            st_ref[hd] = st
            o_s[rows, vs] = _dot_nt(qc[:, ks], st.astype(BF16)) * scale
        return carry

    lax.fori_loop(0, tm // GLA_CHUNK, chunk, 0)

    r = _dot(h, wr_ref[...])
    gated = r * _sigmoid(r)
    gnw = gnw_ref[...]
    parts = []
    for hd in range(GLA_HEADS):
        vs = slice(hd * GLA_DV, (hd + 1) * GLA_DV)
        parts.append((_rms(o_s[:, vs]) * gnw[:, vs] * gated[:, vs]).astype(BF16))
    o_ref[...] = x + _dot(jnp.concatenate(parts, axis=-1), wo_ref[...])


def _gla(x, nw, w_in, w_gate2, b_gate, norm_w, w_out, batch, seq_len, tm=512):
    t = x.shape[0]
    spb = seq_len // tm
    wb = w_in.astype(BF16)
    wq, wk = wb[:, :GLA_HK], wb[:, GLA_HK:2 * GLA_HK]
    wv, wr = wb[:, 2 * GLA_HK:2 * GLA_HK + GLA_HV], wb[:, 2 * GLA_HK + GLA_HV:2 * GLA_HK + 2 * GLA_HV]
    wg1 = jnp.pad(wb[:, 2 * GLA_HK + 2 * GLA_HV:], ((0, 0), (0, 128 - GLA_RANK)))
    wg2 = jnp.pad(w_gate2.astype(BF16), ((0, 128 - GLA_RANK), (0, 0)))
    row = pl.BlockSpec((tm, D_MODEL), lambda b, i: (b * spb + i, 0))
    return pl.pallas_call(
        _gla_kernel,
        grid=(batch, spb),
        in_specs=[row] + [_vmem()] * 10,
        out_specs=row,
        out_shape=jax.ShapeDtypeStruct((t, D_MODEL), F32),
        scratch_shapes=[pltpu.VMEM((GLA_HEADS, GLA_DV, GLA_DK), F32),
                        pltpu.VMEM((tm, GLA_HK), BF16), pltpu.VMEM((tm, GLA_HK), F32),
                        pltpu.VMEM((tm, GLA_HV), BF16), pltpu.VMEM((tm, GLA_HK), F32),
                        pltpu.VMEM((tm, GLA_HV), F32)],
        compiler_params=_params("parallel", "arbitrary"),
        name="gla_mixer",
    )(x, nw.reshape(1, D_MODEL), wq, wk, wv, wr, wg1, wg2, b_gate.reshape(1, GLA_HK),
      norm_w.reshape(1, GLA_HV), w_out.astype(BF16))


def _pool_kernel(x_ref, xp_ref, nw_ref, w_ref, b_ref, sc_ref, o_ref, hs_ref, *, steps_per_seq):
    tm = x_ref.shape[0]
    _normed_with_halo(x_ref, xp_ref, nw_ref, hs_ref, steps_per_seq)
    pos = (pl.program_id(0) % steps_per_seq) * tm + lax.broadcasted_iota(jnp.int32, (tm, 1), 0)
    outs = []
    for gi, win in enumerate(POOL_WINDOWS):
        cols = slice(gi * POOL_GW, (gi + 1) * POOL_GW)
        cur = hs_ref[HALO:, cols]
        total = cur
        for back in range(1, win):
            total = total + hs_ref[HALO - back:HALO - back + tm, cols]
        count = jnp.minimum(pos + 1, win).astype(F32)
        dlt = (total / count - cur).astype(BF16)
        outs.append(_dot(dlt, w_ref[gi]) + b_ref[gi])
    o_ref[...] = x_ref[...] + jnp.concatenate(outs, axis=-1) * sc_ref[...]


def _pool(x, nw, w_grp, b_grp, scale, seq_len, tm=512):
    t = x.shape[0]
    steps_per_seq = seq_len // tm
    row = pl.BlockSpec((tm, D_MODEL), lambda i: (i, 0))
    return pl.pallas_call(
        functools.partial(_pool_kernel, steps_per_seq=steps_per_seq),
        grid=(t // tm,),
        in_specs=[row, _prev_rows_spec(tm, steps_per_seq), _vmem(), _vmem(), _vmem(), _vmem()],
        out_specs=row,
        out_shape=jax.ShapeDtypeStruct((t, D_MODEL), F32),
        scratch_shapes=[pltpu.VMEM((tm + HALO, D_MODEL), F32)],
        compiler_params=_params("parallel"),
        name="pool_mixer",
    )(x, x, nw.reshape(1, D_MODEL), w_grp.astype(BF16), b_grp.reshape(len(POOL_WINDOWS), 1, POOL_GW),
      scale.reshape(1, D_MODEL))


def _ssd_proj_kernel(x_ref, xp_ref, nw_ref, wz_ref, wx_ref, wdt_ref, wdtt_ref, cw_ref, z_ref, xs_ref, bm_ref,
                     cm_ref, dt_ref, dtt_ref, hs_ref, *, steps_per_seq):
    _normed_with_halo(x_ref, xp_ref, nw_ref, hs_ref, steps_per_seq)
    hs = hs_ref[...]
    h = hs_ref[HALO:, :]
    z_ref[...] = _dot(h, wz_ref[...]).astype(z_ref.dtype)
    dt_ref[...] = _dot(h, wdt_ref[...])
    dtt_ref[...] = _dot_nt(wdtt_ref[...], h)
    tile = 512
    for j in range((SSD_DINNER + 2 * SSD_GN) // tile):
        cols = slice(j * tile, (j + 1) * tile)
        u = _dot(hs, wx_ref[:, cols])
        c = cw_ref[:, cols]
        acc = u[HALO:] * c[3:4] + c[4:5]
        for back in range(1, SSD_CONV):
            acc = acc + pltpu.roll(u, back, 0)[HALO:] * c[3 - back:4 - back]
        y = (acc * _sigmoid(acc)).astype(BF16)
        if j < SSD_DINNER // tile:
            xs_ref[:, cols] = y
        elif j == SSD_DINNER // tile:
            bm_ref[...] = y
        else:
            cm_ref[...] = y


def _ssd_core_kernel(x_ref, z_ref, xs_ref, bm_ref, cm_ref, dt_ref, dtt_ref, dtb_r, dtb_c, alog_r, alog_c,
                     dsk_ref, gnw_ref, rexp_ref, wo_ref, o_ref, st_ref, y_s):
    q = SSD_Q

    @pl.when(pl.program_id(1) == 0)
    def _():
        st_ref[...] = jnp.zeros_like(st_ref)

    dt = _softplus(dt_ref[...] + dtb_r[...])
    dtt = _softplus(dtt_ref[...] + dtb_c[...])
    a = dt * -jnp.exp(alog_r[...])
    at = dtt * -jnp.exp(alog_c[...])
    acs = _exact_left(_tri(q, lower=True), a)
    acst = _exact_right(at, _tri(q, lower=False))
    rexp = rexp_ref[...]
    acs_x = _exact_right(acs, rexp)
    dt_x = _exact_right(dt, rexp)
    end_x = acs_x[q - 1:q, :]
    e_x = jnp.exp(acs_x)
    xs = xs_ref[...].astype(F32)
    xdt = xs * dt_x
    w_end = (xdt * jnp.exp(end_x - acs_x)).astype(BF16)
    e_end = jnp.exp(end_x)
    xdt = xdt.astype(BF16)
    causal = (lax.broadcasted_iota(jnp.int32, (q, q), 0) >= lax.broadcasted_iota(jnp.int32, (q, q), 1))
    lane = lax.broadcasted_iota(jnp.int32, (1, 2 * SSD_P), 1)
    gw = SSD_HPG * SSD_P
    for g in range(SSD_GROUPS):
        ns = slice(g * SSD_N, (g + 1) * SSD_N)
        gs = slice(g * gw, (g + 1) * gw)
        bg = bm_ref[:, ns]
        cg = cm_ref[:, ns]
        cb = _dot_nt(cg, bg)
        st = st_ref[g]
        y_s[:, gs] = _dot(cg, st.astype(BF16)) * e_x[:, gs]
        for pair in range(SSD_HPG // 2):
            cols = slice(g * gw + pair * 2 * SSD_P, g * gw + (pair + 1) * 2 * SSD_P)
            xp = xdt[:, cols]
            acc = None
            for half in range(2):
                hh = g * SSD_HPG + pair * 2 + half
                seg = acs[:, hh:hh + 1] - acst[hh:hh + 1, :]
                m = (cb * jnp.where(causal, jnp.exp(jnp.minimum(seg, 0.0)), 0.0)).astype(BF16)
                keep = (lane < SSD_P) if half == 0 else (lane >= SSD_P)
                part = _dot(m, jnp.where(keep, xp, jnp.zeros_like(xp)))
                acc = part if acc is None else acc + part
            y_s[:, cols] += acc
        st_ref[g] = st * e_end[:, gs] + _dot_tn(bg, w_end[:, gs])
    y = y_s[...] + dsk_ref[...] * xs
    zz = z_ref[...].astype(F32)
    y = y * (zz * _sigmoid(zz))
    gnw = gnw_ref[...]
    parts = []
    for g in range(SSD_GROUPS):
        gs = slice(g * gw, (g + 1) * gw)
        parts.append((_rms(y[:, gs]) * gnw[:, gs]).astype(BF16))
    o_ref[...] = x_ref[...] + _dot(jnp.concatenate(parts, axis=-1), wo_ref[...])


def _ssd(x, nw, w_in, conv_w, conv_b, dt_bias, a_log, d_skip, norm_w, w_out, batch, seq_len, tm=512):
    t = x.shape[0]
    steps_per_seq = seq_len // tm
    conv_dim = SSD_DINNER + 2 * SSD_GN
    wb = w_in.astype(BF16)
    wz, wx, wdt = wb[:, :SSD_DINNER], wb[:, SSD_DINNER:SSD_DINNER + conv_dim], wb[:, SSD_DINNER + conv_dim:]
    cw = jnp.concatenate([conv_w, conv_b[None, :], jnp.zeros((3, conv_dim), F32)], axis=0)
    row = pl.BlockSpec((tm, D_MODEL), lambda i: (i, 0))

    def out_row(n):
        return pl.BlockSpec((tm, n), lambda i: (i, 0))

    z, xs, bm, cm, dt, dtt = pl.pallas_call(
        functools.partial(_ssd_proj_kernel, steps_per_seq=steps_per_seq),
        grid=(t // tm,),
        in_specs=[row, _prev_rows_spec(tm, steps_per_seq)] + [_vmem()] * 6,
        out_specs=[out_row(SSD_DINNER), out_row(SSD_DINNER), out_row(SSD_GN), out_row(SSD_GN),
                   out_row(SSD_HEADS), pl.BlockSpec((SSD_HEADS, tm), lambda i: (0, i))],
        out_shape=[jax.ShapeDtypeStruct((t, SSD_DINNER), BF16), jax.ShapeDtypeStruct((t, SSD_DINNER), BF16),
                   jax.ShapeDtypeStruct((t, SSD_GN), BF16), jax.ShapeDtypeStruct((t, SSD_GN), BF16),
                   jax.ShapeDtypeStruct((t, SSD_HEADS), F32), jax.ShapeDtypeStruct((SSD_HEADS, t), F32)],
        scratch_shapes=[pltpu.VMEM((tm + HALO, D_MODEL), BF16)],
        compiler_params=_params("parallel"),
        name="ssd_proj",
    )(x, x, nw.reshape(1, D_MODEL), wz, wx, wdt, wdt.T, cw)

    q = SSD_Q
    spb = seq_len // q
    rexp = jnp.repeat(jnp.eye(SSD_HEADS, dtype=BF16), SSD_P, axis=1)
    dsk = jnp.repeat(d_skip, SSD_P).reshape(1, SSD_DINNER)

    def blk(n):
        return pl.BlockSpec((q, n), lambda b, i: (b * spb + i, 0))

    return pl.pallas_call(
        _ssd_core_kernel,
        grid=(batch, spb),
        in_specs=[blk(D_MODEL), blk(SSD_DINNER), blk(SSD_DINNER), blk(SSD_GN), blk(SSD_GN), blk(SSD_HEADS),
                  pl.BlockSpec((SSD_HEADS, q), lambda b, i: (0, b * spb + i))] + [_vmem()] * 8,
        out_specs=blk(D_MODEL),
        out_shape=jax.ShapeDtypeStruct((t, D_MODEL), F32),
        scratch_shapes=[pltpu.VMEM((SSD_GROUPS, SSD_N, SSD_HPG * SSD_P), F32),
                        pltpu.VMEM((q, SSD_DINNER), F32)],
        compiler_params=_params("parallel", "arbitrary"),
        name="ssd_core",
    )(x, z, xs, bm, cm, dt, dtt, dt_bias.reshape(1, SSD_HEADS), dt_bias.reshape(SSD_HEADS, 1),
      a_log.reshape(1, SSD_HEADS), a_log.reshape(SSD_HEADS, 1), dsk, norm_w.reshape(1, SSD_DINNER), rexp,
      w_out.astype(BF16))


def _sb_proj_kernel(x_ref, nw_ref, wq_ref, wk_ref, wv_ref, q_ref, k_ref, v_ref):
    h = (_rms(x_ref[...]) * nw_ref[...]).astype(BF16)
    q_ref[...] = (_dot(h, wq_ref[...]) * (SB_DH ** -0.5)).astype(BF16)
    k_ref[...] = _dot(h, wk_ref[...]).astype(BF16)
    v_ref[...] = _dot(h, wv_ref[...]).astype(BF16)


def _sb_attn_kernel(q_ref, k_ref, v_ref, o_ref, acc_ref):
    t = SB_T
    i = pl.program_id(2)
    lane = lax.broadcasted_iota(jnp.int32, (1, 2 * SB_DH), 1)
    q = q_ref[0]
    qh = [jnp.where(lane < SB_DH, q, jnp.zeros_like(q)), jnp.where(lane >= SB_DH, q, jnp.zeros_like(q))]
    upper = _tri(t, lower=True)
    strict = (lax.broadcasted_iota(jnp.int32, (t, t), 0) > lax.broadcasted_iota(jnp.int32, (t, t), 1))
    acc_ref[...] = jnp.zeros_like(acc_ref)

    def tile(j, carry, masked):
        rows = pl.ds(pl.multiple_of(j * t, t), t)
        k = k_ref[0, rows, :]
        v = v_ref[0, rows, :]
        new = []
        for hd in range(2):
            z = _dot_nt(qh[hd], k)
            sp = _softplus(z)
            if masked:
                sp = jnp.where(strict, sp, 0.0)
            hi = sp.astype(BF16)
            lo = (sp - hi.astype(F32)).astype(BF16)
            rc = _dot(lo, upper) + _dot(hi, upper)
            att = jnp.exp(z - rc - carry[hd])
            if masked:
                att = jnp.where(strict, att, 0.0)
            acc_ref[hd] += _dot(att.astype(BF16), v)
            new.append(carry[hd] + rc[:, 0:1])
        return tuple(new)

    zero = jnp.zeros((t, 1), F32)
    carry = tile(i, (zero, zero), True)
    lax.fori_loop(0, i, lambda s, c: tile(i - 1 - s, c, False), carry)
    o_ref[0] = jnp.where(lane < SB_DH, acc_ref[0], acc_ref[1]).astype(o_ref.dtype)


def _out_proj_kernel(x_ref, a_ref, w_ref, o_ref):
    o_ref[...] = x_ref[...] + _dot(a_ref[...], w_ref[...])


def _sb(x, nw, w_qkv, w_out, batch, seq_len, tm=512):
    t = x.shape[0]
    wb = w_qkv.astype(BF16)
    row = pl.BlockSpec((tm, D_MODEL), lambda i: (i, 0))
    qkv_shape = jax.ShapeDtypeStruct((t, D_MODEL), BF16)
    q, k, v = pl.pallas_call(
        _sb_proj_kernel,
        grid=(t // tm,),
        in_specs=[row] + [_vmem()] * 4,
        out_specs=[row, row, row],
        out_shape=[qkv_shape, qkv_shape, qkv_shape],
        compiler_params=_params("parallel"),
        name="sb_proj",
    )(x, nw.reshape(1, D_MODEL), wb[:, :D_MODEL], wb[:, D_MODEL:2 * D_MODEL], wb[:, 2 * D_MODEL:])

    shape3 = (batch, seq_len, D_MODEL)
    qblk = pl.BlockSpec((1, SB_T, 2 * SB_DH), lambda b, p, i: (b, i, p))
    kvblk = pl.BlockSpec((1, seq_len, 2 * SB_DH), lambda b, p, i: (b, 0, p))
    att = pl.pallas_call(
        _sb_attn_kernel,
        grid=(batch, SB_HEADS // 2, seq_len // SB_T),
        in_specs=[qblk, kvblk, kvblk],
        out_specs=qblk,
        out_shape=jax.ShapeDtypeStruct(shape3, BF16),
        scratch_shapes=[pltpu.VMEM((2, SB_T, 2 * SB_DH), F32)],
        compiler_params=_params("parallel", "parallel", "arbitrary"),
        name="sb_attention",
    )(q.reshape(shape3), k.reshape(shape3), v.reshape(shape3))

    return pl.pallas_call(
        _out_proj_kernel,
        grid=(t // tm,),
        in_specs=[row, row, _vmem()],
        out_specs=row,
        out_shape=jax.ShapeDtypeStruct((t, D_MODEL), F32),
        compiler_params=_params("parallel"),
        name="sb_out_proj",
    )(x, att.reshape(t, D_MODEL), w_out.astype(BF16))


def kernel(x, mix_norm_w, ffn_norm_w, final_norm_w, gla_w_in, gla_w_gate2, gla_b_gate, gla_norm_w, gla_w_out, pool_w, pool_b, pool_scale, ssd_w_in, ssd_conv_w, ssd_conv_b, ssd_dt_bias, ssd_a_log, ssd_d, ssd_norm_w, ssd_w_out, sb_w_qkv, sb_w_out, ffn_w_up, ffn_conv_w, ffn_conv_b, ffn_w_down):
    batch, seq_len, d = x.shape
    assert d == D_MODEL
    depth = mix_norm_w.shape[0]
    xf = x.reshape(batch * seq_len, d)
    for i in range(depth):
        m, j = i % 4, i // 4
        if m == 0:
            xf = _gla(xf, mix_norm_w[i], gla_w_in[j], gla_w_gate2[j], gla_b_gate[j], gla_norm_w[j], gla_w_out[j],
                      batch, seq_len)
        elif m == 1:
            xf = _pool(xf, mix_norm_w[i], pool_w[j], pool_b[j], pool_scale[j], seq_len)
        elif m == 2:
            xf = _ssd(xf, mix_norm_w[i], ssd_w_in[j], ssd_conv_w[j], ssd_conv_b[j], ssd_dt_bias[j], ssd_a_log[j],
                      ssd_d[j], ssd_norm_w[j], ssd_w_out[j], batch, seq_len)
        else:
            xf = _sb(xf, mix_norm_w[i], sb_w_qkv[j], sb_w_out[j], batch, seq_len)
        xf = _ffn(xf, ffn_norm_w[i], ffn_w_up[i], ffn_conv_w[i], ffn_conv_b[i], ffn_w_down[i], final_norm_w,
                  seq_len, final_norm=(i == depth - 1))
    return xf.reshape(batch, seq_len, d)
```

```python
import functools

import jax
import jax.numpy as jnp
from jax import lax
from jax.experimental import pallas as pl
from jax.experimental.pallas import tpu as pltpu

F32 = jnp.float32
BF16 = jnp.bfloat16

EPS = 1e-6
D_MODEL = 1024
HALO = 16
VMEM_LIMIT = 56 * 1024 * 1024

GLA_HEADS, GLA_DK, GLA_DV, GLA_RANK, GLA_TAU, GLA_CHUNK = 4, 128, 256, 16, 16.0, 64
GLA_HK, GLA_HV = GLA_HEADS * GLA_DK, GLA_HEADS * GLA_DV
POOL_WINDOWS, POOL_GW = (2, 4, 8, 16), 256
SSD_DINNER, SSD_P, SSD_HEADS, SSD_GROUPS, SSD_HPG, SSD_N, SSD_CONV = 2048, 64, 32, 4, 8, 128, 4
SSD_GN = SSD_GROUPS * SSD_N
SSD_Q = 128
SB_HEADS, SB_DH, SB_T = 16, 64, 256
SB_HPS = 4
FFN_DIM, FFN_TILE, FFN_CONV = 2816, 256, 3
FFN_NT = FFN_DIM // FFN_TILE


def _params(*sem):
    return pltpu.CompilerParams(dimension_semantics=sem, vmem_limit_bytes=VMEM_LIMIT)


def _vmem():
    return pl.BlockSpec(memory_space=pltpu.VMEM)


def _rms(x):
    return x * lax.rsqrt(jnp.mean(x * x, axis=-1, keepdims=True) + EPS)


def _softplus(x):
    return jnp.maximum(x, 0.0) + jnp.log(1.0 + jnp.exp(-jnp.abs(x)))


def _sigmoid(x):
    return 1.0 / (1.0 + jnp.exp(-x))


def _dot(a, b):
    return jnp.dot(a, b, preferred_element_type=F32)


def _dot_nt(a, b):
    return lax.dot_general(a, b, (((1,), (1,)), ((), ())), preferred_element_type=F32)


def _dot_tn(a, b):
    return lax.dot_general(a, b, (((0,), (0,)), ((), ())), preferred_element_type=F32)


def _split3(x):
    hi = x.astype(BF16)
    r = x - hi.astype(F32)
    mid = r.astype(BF16)
    lo = (r - mid.astype(F32)).astype(BF16)
    return hi, mid, lo


def _exact_left(m01, x):
    hi, mid, lo = _split3(x)
    return _dot(m01, lo) + _dot(m01, mid) + _dot(m01, hi)


def _exact_right(x, m01):
    hi, mid, lo = _split3(x)
    return _dot(lo, m01) + _dot(mid, m01) + _dot(hi, m01)


def _tri(n, lower):
    r = lax.broadcasted_iota(jnp.int32, (n, n), 0)
    c = lax.broadcasted_iota(jnp.int32, (n, n), 1)
    return jnp.where((r >= c) if lower else (r <= c), 1.0, 0.0).astype(BF16)


def _prev_rows_spec(tm, steps_per_seq):
    del steps_per_seq
    return pl.BlockSpec((HALO, D_MODEL), lambda i: (jnp.maximum(i * (tm // HALO) - 1, 0), 0))


def _normed_with_halo(x_ref, xp_ref, nw_ref, hs_ref, steps_per_seq):
    first = (pl.program_id(0) % steps_per_seq) == 0
    nw = nw_ref[...]
    hp = _rms(xp_ref[...]) * nw
    hs_ref[0:HALO, :] = jnp.where(first, 0.0, hp).astype(hs_ref.dtype)
    hs_ref[HALO:, :] = (_rms(x_ref[...]) * nw).astype(hs_ref.dtype)


def _ffn_kernel(x_ref, xp_ref, nw_ref, wu_ref, cw_ref, wd_ref, fw_ref, o_ref, hs_ref, a_ref,
                *, steps_per_seq, final_norm):
    _normed_with_halo(x_ref, xp_ref, nw_ref, hs_ref, steps_per_seq)

    def up(j):
        hs = hs_ref[...]
        lo = j * FFN_TILE
        return (_dot(hs, wu_ref[:, lo:lo + FFN_TILE]),
                _dot(hs, wu_ref[:, FFN_DIM + lo:FFN_DIM + lo + FFN_TILE]))

    def conv(u, lo):
        c = cw_ref[:, lo:lo + FFN_TILE]
        out = u[HALO:] * c[2:3] + c[3:4]
        out = out + pltpu.roll(u, 1, 0)[HALO:] * c[1:2]
        out = out + pltpu.roll(u, 2, 0)[HALO:] * c[0:1]
        return out

    u = up(0)
    for j in range(FFN_NT):
        nxt = up(j + 1) if j + 1 < FFN_NT else None
        g = conv(u[0], j * FFN_TILE)
        v = conv(u[1], FFN_DIM + j * FFN_TILE)
        a_ref[:, j * FFN_TILE:(j + 1) * FFN_TILE] = (g * _sigmoid(g) * v).astype(BF16)
        u = nxt
    y = x_ref[...] + _dot(a_ref[...], wd_ref[...])
    if final_norm:
        y = _rms(y) * fw_ref[...]
    o_ref[...] = y


def _ffn(x, nw, w_up, conv_w, conv_b, w_down, final_w, seq_len, final_norm, tm=512):
    t = x.shape[0]
    steps_per_seq = seq_len // tm
    cw = jnp.concatenate([conv_w, conv_b[None, :], jnp.zeros((4, 2 * FFN_DIM), F32)], axis=0)
    row = pl.BlockSpec((tm, D_MODEL), lambda i: (i, 0))
    return pl.pallas_call(
        functools.partial(_ffn_kernel, steps_per_seq=steps_per_seq, final_norm=final_norm),
        grid=(t // tm,),
        in_specs=[row, _prev_rows_spec(tm, steps_per_seq)] + [_vmem()] * 5,
        out_specs=row,
        out_shape=jax.ShapeDtypeStruct((t, D_MODEL), F32),
        scratch_shapes=[pltpu.VMEM((tm + HALO, D_MODEL), BF16), pltpu.VMEM((tm, FFN_DIM), BF16)],
        compiler_params=_params("parallel"),
        name="conv_ffn",
    )(x, x, nw.reshape(1, D_MODEL), w_up.astype(BF16), cw, w_down.astype(BF16), final_w.reshape(1, D_MODEL))


def _gla_kernel(x_ref, nw_ref, wq_ref, wk_ref, wv_ref, wr_ref, wg1_ref, wg2_ref, bg_ref, gnw_ref, wo_ref,
                o_ref, st_ref, q_s, k_s, v_s, la_s, o_s):
    tm = x_ref.shape[0]

    @pl.when(pl.program_id(1) == 0)
    def _():
        st_ref[...] = jnp.zeros_like(st_ref)

    x = x_ref[...]
    h = (_rms(x) * nw_ref[...]).astype(BF16)
    q_s[...] = _dot(h, wq_ref[...]).astype(BF16)
    k_s[...] = _dot(h, wk_ref[...])
    v_s[...] = _dot(h, wv_ref[...]).astype(BF16)
    glr = _dot(h, wg1_ref[...]).astype(BF16)
    gate = _dot(glr, wg2_ref[...]) + bg_ref[...]
    la_s[...] = -_softplus(-gate) * (1.0 / GLA_TAU)

    tril = _tri(GLA_CHUNK, lower=True)
    scale = GLA_DK ** -0.5

    def chunk(c, carry):
        rows = pl.ds(pl.multiple_of(c * GLA_CHUNK, GLA_CHUNK), GLA_CHUNK)
        g = _exact_left(tril, la_s[rows, :])
        g_end = g[GLA_CHUNK - 1:GLA_CHUNK, :]
        kd = (k_s[rows, :] * jnp.exp(g_end - g)).astype(BF16)
        eg = jnp.exp(g_end)
        qc = q_s[rows, :]
        vc = v_s[rows, :]
        for hd in range(GLA_HEADS):
            ks = slice(hd * GLA_DK, (hd + 1) * GLA_DK)
            vs = slice(hd * GLA_DV, (hd + 1) * GLA_DV)
            st = st_ref[hd] * eg[:, ks] + _dot_tn(vc[:, vs], kd[:, ks])
            st_ref[hd] = st
            o_s[rows, vs] = _dot_nt(qc[:, ks], st.astype(BF16)) * scale
        return carry

    lax.fori_loop(0, tm // GLA_CHUNK, chunk, 0)

    r = _dot(h, wr_ref[...])
    gated = r * _sigmoid(r)
    gnw = gnw_ref[...]
    parts = []
    for hd in range(GLA_HEADS):
        vs = slice(hd * GLA_DV, (hd + 1) * GLA_DV)
        parts.append((_rms(o_s[:, vs]) * gnw[:, vs] * gated[:, vs]).astype(BF16))
    o_ref[...] = x + _dot(jnp.concatenate(parts, axis=-1), wo_ref[...])


def _gla(x, nw, w_in, w_gate2, b_gate, norm_w, w_out, batch, seq_len, tm=512):
    t = x.shape[0]
    spb = seq_len // tm
    wb = w_in.astype(BF16)
    wq, wk = wb[:, :GLA_HK], wb[:, GLA_HK:2 * GLA_HK]
    wv, wr = wb[:, 2 * GLA_HK:2 * GLA_HK + GLA_HV], wb[:, 2 * GLA_HK + GLA_HV:2 * GLA_HK + 2 * GLA_HV]
    wg1 = jnp.pad(wb[:, 2 * GLA_HK + 2 * GLA_HV:], ((0, 0), (0, 128 - GLA_RANK)))
    wg2 = jnp.pad(w_gate2.astype(BF16), ((0, 128 - GLA_RANK), (0, 0)))
    row = pl.BlockSpec((tm, D_MODEL), lambda b, i: (b * spb + i, 0))
    return pl.pallas_call(
        _gla_kernel,
        grid=(batch, spb),
        in_specs=[row] + [_vmem()] * 10,
        out_specs=row,
        out_shape=jax.ShapeDtypeStruct((t, D_MODEL), F32),
        scratch_shapes=[pltpu.VMEM((GLA_HEADS, GLA_DV, GLA_DK), F32),
                        pltpu.VMEM((tm, GLA_HK), BF16), pltpu.VMEM((tm, GLA_HK), F32),
                        pltpu.VMEM((tm, GLA_HV), BF16), pltpu.VMEM((tm, GLA_HK), F32),
                        pltpu.VMEM((tm, GLA_HV), F32)],
        compiler_params=_params("parallel", "arbitrary"),
        name="gla_mixer",
    )(x, nw.reshape(1, D_MODEL), wq, wk, wv, wr, wg1, wg2, b_gate.reshape(1, GLA_HK),
      norm_w.reshape(1, GLA_HV), w_out.astype(BF16))


def _pool_kernel(x_ref, xp_ref, nw_ref, w_ref, b_ref, sc_ref, o_ref, hs_ref, *, steps_per_seq):
    tm = x_ref.shape[0]
    _normed_with_halo(x_ref, xp_ref, nw_ref, hs_ref, steps_per_seq)
    pos = (pl.program_id(0) % steps_per_seq) * tm + lax.broadcasted_iota(jnp.int32, (tm, 1), 0)
    outs = []
    for gi, win in enumerate(POOL_WINDOWS):
        cols = slice(gi * POOL_GW, (gi + 1) * POOL_GW)
        cur = hs_ref[HALO:, cols]
        total = cur
        for back in range(1, win):
            total = total + hs_ref[HALO - back:HALO - back + tm, cols]
        count = jnp.minimum(pos + 1, win).astype(F32)
        dlt = (total / count - cur).astype(BF16)
        outs.append(_dot(dlt, w_ref[gi]) + b_ref[gi])
    o_ref[...] = x_ref[...] + jnp.concatenate(outs, axis=-1) * sc_ref[...]


def _pool(x, nw, w_grp, b_grp, scale, seq_len, tm=512):
    t = x.shape[0]
    steps_per_seq = seq_len // tm
    row = pl.BlockSpec((tm, D_MODEL), lambda i: (i, 0))
    return pl.pallas_call(
        functools.partial(_pool_kernel, steps_per_seq=steps_per_seq),
        grid=(t // tm,),
        in_specs=[row, _prev_rows_spec(tm, steps_per_seq), _vmem(), _vmem(), _vmem(), _vmem()],
        out_specs=row,
        out_shape=jax.ShapeDtypeStruct((t, D_MODEL), F32),
        scratch_shapes=[pltpu.VMEM((tm + HALO, D_MODEL), F32)],
        compiler_params=_params("parallel"),
        name="pool_mixer",
    )(x, x, nw.reshape(1, D_MODEL), w_grp.astype(BF16), b_grp.reshape(len(POOL_WINDOWS), 1, POOL_GW),
      scale.reshape(1, D_MODEL))


def _ssd_proj_kernel(x_ref, xp_ref, nw_ref, wz_ref, wx_ref, wdt_ref, wdtt_ref, cw_ref, z_ref, xs_ref, bm_ref,
                     cm_ref, dt_ref, dtt_ref, hs_ref, *, steps_per_seq):
    _normed_with_halo(x_ref, xp_ref, nw_ref, hs_ref, steps_per_seq)
    hs = hs_ref[...]
    h = hs_ref[HALO:, :]
    z_ref[...] = _dot(h, wz_ref[...]).astype(z_ref.dtype)
    dt_ref[...] = _dot(h, wdt_ref[...])
    dtt_ref[...] = _dot_nt(wdtt_ref[...], h)
    tile = 512
    for j in range((SSD_DINNER + 2 * SSD_GN) // tile):
        cols = slice(j * tile, (j + 1) * tile)
        u = _dot(hs, wx_ref[:, cols])
        c = cw_ref[:, cols]
        acc = u[HALO:] * c[3:4] + c[4:5]
        for back in range(1, SSD_CONV):
            acc = acc + pltpu.roll(u, back, 0)[HALO:] * c[3 - back:4 - back]
        y = (acc * _sigmoid(acc)).astype(BF16)
        if j < SSD_DINNER // tile:
            xs_ref[:, cols] = y
        elif j == SSD_DINNER // tile:
            bm_ref[...] = y
        else:
            cm_ref[...] = y


def _ssd_core_kernel(x_ref, z_ref, xs_ref, bm_ref, cm_ref, dt_ref, dtt_ref, dtb_r, dtb_c, alog_r, alog_c,
                     dsk_ref, gnw_ref, rexp_ref, wo_ref, o_ref, st_ref, y_s):
    q = SSD_Q

    @pl.when(pl.program_id(1) == 0)
    def _():
        st_ref[...] = jnp.zeros_like(st_ref)

    dt = _softplus(dt_ref[...] + dtb_r[...])
    dtt = _softplus(dtt_ref[...] + dtb_c[...])
    a = dt * -jnp.exp(alog_r[...])
    at = dtt * -jnp.exp(alog_c[...])
    acs = _exact_left(_tri(q, lower=True), a)
    acst = _exact_right(at, _tri(q, lower=False))
    rexp = rexp_ref[...]
    acs_x = _exact_right(acs, rexp)
    dt_x = _exact_right(dt, rexp)
    end_x = acs_x[q - 1:q, :]
    e_x = jnp.exp(acs_x)
    xs = xs_ref[...].astype(F32)
    xdt = xs * dt_x
    w_end = (xdt * jnp.exp(end_x - acs_x)).astype(BF16)
    e_end = jnp.exp(end_x)
    xdt = xdt.astype(BF16)
    causal = (lax.broadcasted_iota(jnp.int32, (q, q), 0) >= lax.broadcasted_iota(jnp.int32, (q, q), 1))
    lane = lax.broadcasted_iota(jnp.int32, (1, 2 * SSD_P), 1)
    gw = SSD_HPG * SSD_P
    for g in range(SSD_GROUPS):
        ns = slice(g * SSD_N, (g + 1) * SSD_N)
        gs = slice(g * gw, (g + 1) * gw)
        bg = bm_ref[:, ns]
        cg = cm_ref[:, ns]
        cb = _dot_nt(cg, bg)
        st = st_ref[g]
        y_s[:, gs] = _dot(cg, st.astype(BF16)) * e_x[:, gs]
        for pair in range(SSD_HPG // 2):
            cols = slice(g * gw + pair * 2 * SSD_P, g * gw + (pair + 1) * 2 * SSD_P)
            xp = xdt[:, cols]
            acc = None
            for half in range(2):
                hh = g * SSD_HPG + pair * 2 + half
                seg = acs[:, hh:hh + 1] - acst[hh:hh + 1, :]
                m = (cb * jnp.where(causal, jnp.exp(jnp.minimum(seg, 0.0)), 0.0)).astype(BF16)
                keep = (lane < SSD_P) if half == 0 else (lane >= SSD_P)
                part = _dot(m, jnp.where(keep, xp, jnp.zeros_like(xp)))
                acc = part if acc is None else acc + part
            y_s[:, cols] += acc
        st_ref[g] = st * e_end[:, gs] + _dot_tn(bg, w_end[:, gs])
    y = y_s[...] + dsk_ref[...] * xs
    zz = z_ref[...].astype(F32)
    y = y * (zz * _sigmoid(zz))
    gnw = gnw_ref[...]
    parts = []
    for g in range(SSD_GROUPS):
        gs = slice(g * gw, (g + 1) * gw)
        parts.append((_rms(y[:, gs]) * gnw[:, gs]).astype(BF16))
    o_ref[...] = x_ref[...] + _dot(jnp.concatenate(parts, axis=-1), wo_ref[...])


def _ssd(x, nw, w_in, conv_w, conv_b, dt_bias, a_log, d_skip, norm_w, w_out, batch, seq_len, tm=512):
    t = x.shape[0]
    steps_per_seq = seq_len // tm
    conv_dim = SSD_DINNER + 2 * SSD_GN
    wb = w_in.astype(BF16)
    wz, wx, wdt = wb[:, :SSD_DINNER], wb[:, SSD_DINNER:SSD_DINNER + conv_dim], wb[:, SSD_DINNER + conv_dim:]
    cw = jnp.concatenate([conv_w, conv_b[None, :], jnp.zeros((3, conv_dim), F32)], axis=0)
    row = pl.BlockSpec((tm, D_MODEL), lambda i: (i, 0))

    def out_row(n):
        return pl.BlockSpec((tm, n), lambda i: (i, 0))

    z, xs, bm, cm, dt, dtt = pl.pallas_call(
        functools.partial(_ssd_proj_kernel, steps_per_seq=steps_per_seq),
        grid=(t // tm,),
        in_specs=[row, _prev_rows_spec(tm, steps_per_seq)] + [_vmem()] * 6,
        out_specs=[out_row(SSD_DINNER), out_row(SSD_DINNER), out_row(SSD_GN), out_row(SSD_GN),
                   out_row(SSD_HEADS), pl.BlockSpec((SSD_HEADS, tm), lambda i: (0, i))],
        out_shape=[jax.ShapeDtypeStruct((t, SSD_DINNER), BF16), jax.ShapeDtypeStruct((t, SSD_DINNER), BF16),
                   jax.ShapeDtypeStruct((t, SSD_GN), BF16), jax.ShapeDtypeStruct((t, SSD_GN), BF16),
                   jax.ShapeDtypeStruct((t, SSD_HEADS), F32), jax.ShapeDtypeStruct((SSD_HEADS, t), F32)],
        scratch_shapes=[pltpu.VMEM((tm + HALO, D_MODEL), BF16)],
        compiler_params=_params("parallel"),
        name="ssd_proj",
    )(x, x, nw.reshape(1, D_MODEL), wz, wx, wdt, wdt.T, cw)

    q = SSD_Q
    spb = seq_len // q
    rexp = jnp.repeat(jnp.eye(SSD_HEADS, dtype=BF16), SSD_P, axis=1)
    dsk = jnp.repeat(d_skip, SSD_P).reshape(1, SSD_DINNER)

    def blk(n):
        return pl.BlockSpec((q, n), lambda b, i: (b * spb + i, 0))

    return pl.pallas_call(
        _ssd_core_kernel,
        grid=(batch, spb),
        in_specs=[blk(D_MODEL), blk(SSD_DINNER), blk(SSD_DINNER), blk(SSD_GN), blk(SSD_GN), blk(SSD_HEADS),
                  pl.BlockSpec((SSD_HEADS, q), lambda b, i: (0, b * spb + i))] + [_vmem()] * 8,
        out_specs=blk(D_MODEL),
        out_shape=jax.ShapeDtypeStruct((t, D_MODEL), F32),
        scratch_shapes=[pltpu.VMEM((SSD_GROUPS, SSD_N, SSD_HPG * SSD_P), F32),
                        pltpu.VMEM((q, SSD_DINNER), F32)],
        compiler_params=_params("parallel", "arbitrary"),
        name="ssd_core",
    )(x, z, xs, bm, cm, dt, dtt, dt_bias.reshape(1, SSD_HEADS), dt_bias.reshape(SSD_HEADS, 1),
      a_log.reshape(1, SSD_HEADS), a_log.reshape(SSD_HEADS, 1), dsk, norm_w.reshape(1, SSD_DINNER), rexp,
      w_out.astype(BF16))


def _sb_proj_kernel(x_ref, nw_ref, wq_ref, wk_ref, wv_ref, q_ref, k_ref, v_ref):
    h = (_rms(x_ref[...]) * nw_ref[...]).astype(BF16)
    q_ref[...] = (_dot(h, wq_ref[...]) * (SB_DH ** -0.5)).astype(BF16)
    k_ref[...] = _dot(h, wk_ref[...]).astype(BF16)
    v_ref[...] = _dot(h, wv_ref[...]).astype(BF16)


def _sb_attn_kernel(q_ref, k_ref, v_ref, o_ref, acc_ref, z_s, a_s):
    t, nh = SB_T, SB_HPS
    i = pl.program_id(2)
    head_of_lane = lax.broadcasted_iota(jnp.int32, (1, nh * SB_DH), 1) // SB_DH

    def per_head(a):
        return jnp.concatenate([jnp.where(head_of_lane == h, a, jnp.zeros_like(a)) for h in range(nh)], axis=0)

    def key_rows(j):
        return pl.ds(pl.multiple_of(j * t, t), t)

    qs = per_head(q_ref[0])
    upper = _tri(t, lower=True)
    strict = (lax.broadcasted_iota(jnp.int32, (t, t), 0) > lax.broadcasted_iota(jnp.int32, (t, t), 1))

    def logits(j):
        return _dot_nt(qs, k_ref[0, key_rows(j), :])

    def survival(z_of, carry, masked):
        rcs, ys = [], []
        for h in range(nh):
            z = z_of(h)
            sp = _softplus(z)
            if masked:
                sp = jnp.where(strict, sp, 0.0)
            hi = sp.astype(BF16)
            lo = (sp - hi.astype(F32)).astype(BF16)
            r2 = _dot(jnp.concatenate([hi, lo], axis=0), upper)
            rcs.append(r2[:t] + r2[t:])
            ys.append(z - carry[h])
        return rcs, ys

    def weights(rcs, ys, masked):
        for h in range(nh):
            att = jnp.exp(ys[h] - rcs[h])
            if masked:
                att = jnp.where(strict, att, 0.0)
            a_s[:, h * t:(h + 1) * t] = att.astype(BF16)

    def weighted_values(j):
        return _dot(a_s[...], per_head(v_ref[0, key_rows(j), :]))

    zero = jnp.zeros((t, 1), F32)
    zz = logits(i)
    rcs, ys = survival(lambda h: zz[h * t:(h + 1) * t], (zero,) * nh, True)
    z_s[...] = logits(jnp.maximum(i - 1, 0))
    weights(rcs, ys, True)
    acc_ref[...] = jnp.zeros_like(acc_ref)

    def step(s, carry):
        j = i - 1 - s
        pv = weighted_values(j + 1)
        rcs, ys = survival(lambda h: z_s[h * t:(h + 1) * t, :], carry, False)
        acc_ref[...] += pv
        z_s[...] = logits(jnp.maximum(j - 1, 0))
        weights(rcs, ys, False)
        return tuple(carry[h] + rcs[h][:, 0:1] for h in range(nh))

    lax.fori_loop(0, i, step, tuple(rcs[h][:, 0:1] for h in range(nh)))
    o_ref[0] = (acc_ref[...] + weighted_values(0)).astype(o_ref.dtype)


def _out_proj_kernel(x_ref, a_ref, w_ref, o_ref):
    o_ref[...] = x_ref[...] + _dot(a_ref[...], w_ref[...])


def _sb(x, nw, w_qkv, w_out, batch, seq_len, tm=512):
    t = x.shape[0]
    wb = w_qkv.astype(BF16)
    row = pl.BlockSpec((tm, D_MODEL), lambda i: (i, 0))
    qkv_shape = jax.ShapeDtypeStruct((t, D_MODEL), BF16)
    q, k, v = pl.pallas_call(
        _sb_proj_kernel,
        grid=(t // tm,),
        in_specs=[row] + [_vmem()] * 4,
        out_specs=[row, row, row],
        out_shape=[qkv_shape, qkv_shape, qkv_shape],
        compiler_params=_params("parallel"),
        name="sb_proj",
    )(x, nw.reshape(1, D_MODEL), wb[:, :D_MODEL], wb[:, D_MODEL:2 * D_MODEL], wb[:, 2 * D_MODEL:])

    shape3 = (batch, seq_len, D_MODEL)
    lanes = SB_HPS * SB_DH
    qblk = pl.BlockSpec((1, SB_T, lanes), lambda b, p, i: (b, i, p))
    kvblk = pl.BlockSpec((1, seq_len, lanes), lambda b, p, i: (b, 0, p))
    att = pl.pallas_call(
        _sb_attn_kernel,
        grid=(batch, SB_HEADS // SB_HPS, seq_len // SB_T),
        in_specs=[qblk, kvblk, kvblk],
        out_specs=qblk,
        out_shape=jax.ShapeDtypeStruct(shape3, BF16),
        scratch_shapes=[pltpu.VMEM((SB_T, lanes), F32), pltpu.VMEM((SB_HPS * SB_T, SB_T), F32),
                        pltpu.VMEM((SB_T, SB_HPS * SB_T), BF16)],
        compiler_params=_params("parallel", "parallel", "arbitrary"),
        name="sb_attention",
    )(q.reshape(shape3), k.reshape(shape3), v.reshape(shape3))

    return pl.pallas_call(
        _out_proj_kernel,
        grid=(t // tm,),
        in_specs=[row, row, _vmem()],
        out_specs=row,
        out_shape=jax.ShapeDtypeStruct((t, D_MODEL), F32),
        compiler_params=_params("parallel"),
        name="sb_out_proj",
    )(x, att.reshape(t, D_MODEL), w_out.astype(BF16))


def kernel(x, mix_norm_w, ffn_norm_w, final_norm_w, gla_w_in, gla_w_gate2, gla_b_gate, gla_norm_w, gla_w_out, pool_w, pool_b, pool_scale, ssd_w_in, ssd_conv_w, ssd_conv_b, ssd_dt_bias, ssd_a_log, ssd_d, ssd_norm_w, ssd_w_out, sb_w_qkv, sb_w_out, ffn_w_up, ffn_conv_w, ffn_conv_b, ffn_w_down):
    batch, seq_len, d = x.shape
    assert d == D_MODEL
    depth = mix_norm_w.shape[0]
    xf = x.reshape(batch * seq_len, d)
    for i in range(depth):
        m, j = i % 4, i // 4
        if m == 0:
            xf = _gla(xf, mix_norm_w[i], gla_w_in[j], gla_w_gate2[j], gla_b_gate[j], gla_norm_w[j], gla_w_out[j],
                      batch, seq_len)
        elif m == 1:
            xf = _pool(xf, mix_norm_w[i], pool_w[j], pool_b[j], pool_scale[j], seq_len)
        elif m == 2:
            xf = _ssd(xf, mix_norm_w[i], ssd_w_in[j], ssd_conv_w[j], ssd_conv_b[j], ssd_dt_bias[j], ssd_a_log[j],
                      ssd_d[j], ssd_norm_w[j], ssd_w_out[j], batch, seq_len)
        else:
            xf = _sb(xf, mix_norm_w[i], sb_w_qkv[j], sb_w_out[j], batch, seq_len)
        xf = _ffn(xf, ffn_norm_w[i], ffn_w_up[i], ffn_conv_w[i], ffn_conv_b[i], ffn_w_down[i], final_norm_w,
                  seq_len, final_norm=(i == depth - 1))
    return xf.reshape(batch, seq_len, d)
```

```python
import functools

import jax
import jax.numpy as jnp
from jax import lax
from jax.experimental import pallas as pl
from jax.experimental.pallas import tpu as pltpu

F32 = jnp.float32
BF16 = jnp.bfloat16

EPS = 1e-6
LOG2E = 1.4426950408889634
D_MODEL = 1024
HALO = 16
VMEM_LIMIT = 56 * 1024 * 1024

GLA_HEADS, GLA_DK, GLA_DV, GLA_RANK, GLA_TAU, GLA_CHUNK = 4, 128, 256, 16, 16.0, 64
GLA_HK, GLA_HV = GLA_HEADS * GLA_DK, GLA_HEADS * GLA_DV
POOL_WINDOWS, POOL_GW = (2, 4, 8, 16), 256
SSD_DINNER, SSD_P, SSD_HEADS, SSD_GROUPS, SSD_HPG, SSD_N, SSD_CONV = 2048, 64, 32, 4, 8, 128, 4
SSD_GN = SSD_GROUPS * SSD_N
SSD_Q = 128
SB_HEADS, SB_DH, SB_T = 16, 64, 256
SB_HPS = 4
FFN_DIM, FFN_TILE, FFN_CONV = 2816, 256, 3
FFN_NT = FFN_DIM // FFN_TILE


def _params(*sem):
    return pltpu.CompilerParams(dimension_semantics=sem, vmem_limit_bytes=VMEM_LIMIT)


def _vmem():
    return pl.BlockSpec(memory_space=pltpu.VMEM)


def _rms(x):
    return x * lax.rsqrt(jnp.mean(x * x, axis=-1, keepdims=True) + EPS)


def _softplus(x):
    return jnp.maximum(x, 0.0) + jnp.log(1.0 + jnp.exp(-jnp.abs(x)))


def _softplus_base2(x):
    neg_abs = lax.bitcast_convert_type(lax.bitcast_convert_type(x, jnp.int32) | jnp.int32(-2 ** 31), F32)
    return jnp.maximum(x, 0.0) + jnp.log(1.0 + jnp.exp2(neg_abs)) * LOG2E


def _sigmoid(x):
    return 1.0 / (1.0 + jnp.exp(-x))


def _dot(a, b):
    return jnp.dot(a, b, preferred_element_type=F32)


def _dot_nt(a, b):
    return lax.dot_general(a, b, (((1,), (1,)), ((), ())), preferred_element_type=F32)


def _dot_tn(a, b):
    return lax.dot_general(a, b, (((0,), (0,)), ((), ())), preferred_element_type=F32)


def _split3(x):
    hi = x.astype(BF16)
    r = x - hi.astype(F32)
    mid = r.astype(BF16)
    lo = (r - mid.astype(F32)).astype(BF16)
    return hi, mid, lo


def _exact_left(m01, x):
    hi, mid, lo = _split3(x)
    return _dot(m01, lo) + _dot(m01, mid) + _dot(m01, hi)


def _exact_right(x, m01):
    hi, mid, lo = _split3(x)
    return _dot(lo, m01) + _dot(mid, m01) + _dot(hi, m01)


def _tri(n, lower):
    r = lax.broadcasted_iota(jnp.int32, (n, n), 0)
    c = lax.broadcasted_iota(jnp.int32, (n, n), 1)
    return jnp.where((r >= c) if lower else (r <= c), 1.0, 0.0).astype(BF16)


def _prev_rows_spec(tm, steps_per_seq):
    del steps_per_seq
    return pl.BlockSpec((HALO, D_MODEL), lambda i: (jnp.maximum(i * (tm // HALO) - 1, 0), 0))


def _normed_with_halo(x_ref, xp_ref, nw_ref, hs_ref, steps_per_seq):
    first = (pl.program_id(0) % steps_per_seq) == 0
    nw = nw_ref[...]
    hp = _rms(xp_ref[...]) * nw
    hs_ref[0:HALO, :] = jnp.where(first, 0.0, hp).astype(hs_ref.dtype)
    hs_ref[HALO:, :] = (_rms(x_ref[...]) * nw).astype(hs_ref.dtype)


def _ffn_kernel(x_ref, xp_ref, nw_ref, wu_ref, cw_ref, wd_ref, fw_ref, o_ref, hs_ref, a_ref,
                *, steps_per_seq, final_norm):
    _normed_with_halo(x_ref, xp_ref, nw_ref, hs_ref, steps_per_seq)

    def up(j):
        hs = hs_ref[...]
        lo = j * FFN_TILE
        return (_dot(hs, wu_ref[:, lo:lo + FFN_TILE]),
                _dot(hs, wu_ref[:, FFN_DIM + lo:FFN_DIM + lo + FFN_TILE]))

    def conv(u, lo):
        c = cw_ref[:, lo:lo + FFN_TILE]
        out = u[HALO:] * c[2:3] + c[3:4]
        out = out + pltpu.roll(u, 1, 0)[HALO:] * c[1:2]
        out = out + pltpu.roll(u, 2, 0)[HALO:] * c[0:1]
        return out

    u = up(0)
    for j in range(FFN_NT):
        nxt = up(j + 1) if j + 1 < FFN_NT else None
        g = conv(u[0], j * FFN_TILE)
        v = conv(u[1], FFN_DIM + j * FFN_TILE)
        a_ref[:, j * FFN_TILE:(j + 1) * FFN_TILE] = (g * _sigmoid(g) * v).astype(BF16)
        u = nxt
    y = x_ref[...] + _dot(a_ref[...], wd_ref[...])
    if final_norm:
        y = _rms(y) * fw_ref[...]
    o_ref[...] = y


def _ffn(x, nw, w_up, conv_w, conv_b, w_down, final_w, seq_len, final_norm, tm=512):
    t = x.shape[0]
    steps_per_seq = seq_len // tm
    cw = jnp.concatenate([conv_w, conv_b[None, :], jnp.zeros((4, 2 * FFN_DIM), F32)], axis=0)
    row = pl.BlockSpec((tm, D_MODEL), lambda i: (i, 0))
    return pl.pallas_call(
        functools.partial(_ffn_kernel, steps_per_seq=steps_per_seq, final_norm=final_norm),
        grid=(t // tm,),
        in_specs=[row, _prev_rows_spec(tm, steps_per_seq)] + [_vmem()] * 5,
        out_specs=row,
        out_shape=jax.ShapeDtypeStruct((t, D_MODEL), F32),
        scratch_shapes=[pltpu.VMEM((tm + HALO, D_MODEL), BF16), pltpu.VMEM((tm, FFN_DIM), BF16)],
        compiler_params=_params("parallel"),
        name="conv_ffn",
    )(x, x, nw.reshape(1, D_MODEL), w_up.astype(BF16), cw, w_down.astype(BF16), final_w.reshape(1, D_MODEL))


def _gla_kernel(x_ref, nw_ref, wq_ref, wk_ref, wv_ref, wr_ref, wg1_ref, wg2_ref, bg_ref, gnw_ref, wo_ref,
                o_ref, st_ref, q_s, k_s, v_s, la_s, o_s):
    tm = x_ref.shape[0]

    @pl.when(pl.program_id(1) == 0)
    def _():
        st_ref[...] = jnp.zeros_like(st_ref)

    x = x_ref[...]
    h = (_rms(x) * nw_ref[...]).astype(BF16)
    q_s[...] = _dot(h, wq_ref[...]).astype(BF16)
    k_s[...] = _dot(h, wk_ref[...])
    v_s[...] = _dot(h, wv_ref[...]).astype(BF16)
    glr = _dot(h, wg1_ref[...]).astype(BF16)
    gate = _dot(glr, wg2_ref[...]) + bg_ref[...]
    la_s[...] = -_softplus(-gate) * (1.0 / GLA_TAU)

    tril = _tri(GLA_CHUNK, lower=True)
    scale = GLA_DK ** -0.5
    nchunks = tm // GLA_CHUNK
    k_cols = [slice(hd * GLA_DK, (hd + 1) * GLA_DK) for hd in range(GLA_HEADS)]
    v_cols = [slice(hd * GLA_DV, (hd + 1) * GLA_DV) for hd in range(GLA_HEADS)]

    def chunk_update(c):
        rows = slice(c * GLA_CHUNK, (c + 1) * GLA_CHUNK)
        g = _exact_left(tril, la_s[rows, :])
        g_end = g[GLA_CHUNK - 1:GLA_CHUNK, :]
        kd = (k_s[rows, :] * jnp.exp(g_end - g)).astype(BF16)
        vc = v_s[rows, :]
        return [_dot_tn(vc[:, v_cols[hd]], kd[:, k_cols[hd]]) for hd in range(GLA_HEADS)], jnp.exp(g_end)

    state = [st_ref[hd] for hd in range(GLA_HEADS)]
    pending = chunk_update(0)
    for c in range(nchunks):
        upd, eg = pending
        if c + 1 < nchunks:
            pending = chunk_update(c + 1)
        rows = slice(c * GLA_CHUNK, (c + 1) * GLA_CHUNK)
        qc = q_s[rows, :]
        for hd in range(GLA_HEADS):
            state[hd] = state[hd] * eg[:, k_cols[hd]] + upd[hd]
            o_s[rows, v_cols[hd]] = _dot_nt(qc[:, k_cols[hd]], state[hd].astype(BF16)) * scale
    for hd in range(GLA_HEADS):
        st_ref[hd] = state[hd]

    r = _dot(h, wr_ref[...])
    gated = r * _sigmoid(r)
    gnw = gnw_ref[...]
    parts = []
    for hd in range(GLA_HEADS):
        vs = slice(hd * GLA_DV, (hd + 1) * GLA_DV)
        parts.append((_rms(o_s[:, vs]) * gnw[:, vs] * gated[:, vs]).astype(BF16))
    o_ref[...] = x + _dot(jnp.concatenate(parts, axis=-1), wo_ref[...])


def _gla(x, nw, w_in, w_gate2, b_gate, norm_w, w_out, batch, seq_len, tm=512):
    t = x.shape[0]
    spb = seq_len // tm
    wb = w_in.astype(BF16)
    wq, wk = wb[:, :GLA_HK], wb[:, GLA_HK:2 * GLA_HK]
    wv, wr = wb[:, 2 * GLA_HK:2 * GLA_HK + GLA_HV], wb[:, 2 * GLA_HK + GLA_HV:2 * GLA_HK + 2 * GLA_HV]
    wg1 = jnp.pad(wb[:, 2 * GLA_HK + 2 * GLA_HV:], ((0, 0), (0, 128 - GLA_RANK)))
    wg2 = jnp.pad(w_gate2.astype(BF16), ((0, 128 - GLA_RANK), (0, 0)))
    row = pl.BlockSpec((tm, D_MODEL), lambda b, i: (b * spb + i, 0))
    return pl.pallas_call(
        _gla_kernel,
        grid=(batch, spb),
        in_specs=[row] + [_vmem()] * 10,
        out_specs=row,
        out_shape=jax.ShapeDtypeStruct((t, D_MODEL), F32),
        scratch_shapes=[pltpu.VMEM((GLA_HEADS, GLA_DV, GLA_DK), F32),
                        pltpu.VMEM((tm, GLA_HK), BF16), pltpu.VMEM((tm, GLA_HK), F32),
                        pltpu.VMEM((tm, GLA_HV), BF16), pltpu.VMEM((tm, GLA_HK), F32),
                        pltpu.VMEM((tm, GLA_HV), F32)],
        compiler_params=_params("parallel", "arbitrary"),
        name="gla_mixer",
    )(x, nw.reshape(1, D_MODEL), wq, wk, wv, wr, wg1, wg2, b_gate.reshape(1, GLA_HK),
      norm_w.reshape(1, GLA_HV), w_out.astype(BF16))


def _pool_kernel(x_ref, xp_ref, nw_ref, w_ref, b_ref, sc_ref, o_ref, hs_ref, *, steps_per_seq):
    tm = x_ref.shape[0]
    _normed_with_halo(x_ref, xp_ref, nw_ref, hs_ref, steps_per_seq)
    pos = (pl.program_id(0) % steps_per_seq) * tm + lax.broadcasted_iota(jnp.int32, (tm, 1), 0)
    outs = []
    for gi, win in enumerate(POOL_WINDOWS):
        cols = slice(gi * POOL_GW, (gi + 1) * POOL_GW)
        cur = hs_ref[HALO:, cols]
        total = cur
        for back in range(1, win):
            total = total + hs_ref[HALO - back:HALO - back + tm, cols]
        count = jnp.minimum(pos + 1, win).astype(F32)
        dlt = (total / count - cur).astype(BF16)
        outs.append(_dot(dlt, w_ref[gi]) + b_ref[gi])
    o_ref[...] = x_ref[...] + jnp.concatenate(outs, axis=-1) * sc_ref[...]


def _pool(x, nw, w_grp, b_grp, scale, seq_len, tm=512):
    t = x.shape[0]
    steps_per_seq = seq_len // tm
    row = pl.BlockSpec((tm, D_MODEL), lambda i: (i, 0))
    return pl.pallas_call(
        functools.partial(_pool_kernel, steps_per_seq=steps_per_seq),
        grid=(t // tm,),
        in_specs=[row, _prev_rows_spec(tm, steps_per_seq), _vmem(), _vmem(), _vmem(), _vmem()],
        out_specs=row,
        out_shape=jax.ShapeDtypeStruct((t, D_MODEL), F32),
        scratch_shapes=[pltpu.VMEM((tm + HALO, D_MODEL), F32)],
        compiler_params=_params("parallel"),
        name="pool_mixer",
    )(x, x, nw.reshape(1, D_MODEL), w_grp.astype(BF16), b_grp.reshape(len(POOL_WINDOWS), 1, POOL_GW),
      scale.reshape(1, D_MODEL))


def _ssd_proj_kernel(x_ref, xp_ref, nw_ref, wz_ref, wx_ref, wdt_ref, wdtt_ref, cw_ref, z_ref, xs_ref, bm_ref,
                     cm_ref, dt_ref, dtt_ref, hs_ref, *, steps_per_seq):
    _normed_with_halo(x_ref, xp_ref, nw_ref, hs_ref, steps_per_seq)
    h = hs_ref[HALO:, :]
    tile = 256
    ntiles = (SSD_DINNER + 2 * SSD_GN) // tile

    def up(j):
        return _dot(hs_ref[...], wx_ref[:, j * tile:(j + 1) * tile])

    u = up(0)
    for j in range(ntiles):
        cols = slice(j * tile, (j + 1) * tile)
        nxt = up(j + 1) if j + 1 < ntiles else _dot(h, wz_ref[...])
        c = cw_ref[:, cols]
        acc = u[HALO:] * c[3:4] + c[4:5]
        for back in range(1, SSD_CONV):
            acc = acc + pltpu.roll(u, back, 0)[HALO:] * c[3 - back:4 - back]
        y = (acc * _sigmoid(acc)).astype(BF16)
        lo = j * tile
        if lo < SSD_DINNER:
            xs_ref[:, cols] = y
        elif lo < SSD_DINNER + SSD_GN:
            bm_ref[:, lo - SSD_DINNER:lo - SSD_DINNER + tile] = y
        else:
            cm_ref[:, lo - SSD_DINNER - SSD_GN:lo - SSD_DINNER - SSD_GN + tile] = y
        u = nxt
    z_ref[...] = u.astype(z_ref.dtype)
    dt_ref[...] = _dot(h, wdt_ref[...])
    dtt_ref[...] = _dot_nt(wdtt_ref[...], h)


def _ssd_core_kernel(x_ref, z_ref, xs_ref, bm_ref, cm_ref, dt_ref, dtt_ref, dtb_r, dtb_c, alog_r, alog_c,
                     dsk_ref, gnw_ref, rexp_ref, wo_ref, o_ref, st_ref, y_s):
    q = SSD_Q

    @pl.when(pl.program_id(1) == 0)
    def _():
        st_ref[...] = jnp.zeros_like(st_ref)

    dt = _softplus(dt_ref[...] + dtb_r[...])
    dtt = _softplus(dtt_ref[...] + dtb_c[...])
    a = dt * -jnp.exp(alog_r[...])
    at = dtt * -jnp.exp(alog_c[...])
    acs = _exact_left(_tri(q, lower=True), a)
    acst = _exact_right(at, _tri(q, lower=False))
    rexp = rexp_ref[...]
    acs_x = _exact_right(acs, rexp)
    dt_x = _exact_right(dt, rexp)
    end_x = acs_x[q - 1:q, :]
    e_x = jnp.exp(acs_x)
    xs = xs_ref[...].astype(F32)
    xdt = xs * dt_x
    w_end = (xdt * jnp.exp(end_x - acs_x)).astype(BF16)
    e_end = jnp.exp(end_x)
    xdt = xdt.astype(BF16)
    causal = (lax.broadcasted_iota(jnp.int32, (q, q), 0) >= lax.broadcasted_iota(jnp.int32, (q, q), 1))
    lane = lax.broadcasted_iota(jnp.int32, (1, 2 * SSD_P), 1)
    gw = SSD_HPG * SSD_P
    for g in range(SSD_GROUPS):
        ns = slice(g * SSD_N, (g + 1) * SSD_N)
        gs = slice(g * gw, (g + 1) * gw)
        bg = bm_ref[:, ns]
        cg = cm_ref[:, ns]
        cb = _dot_nt(cg, bg)
        st = st_ref[g]
        y_s[:, gs] = _dot(cg, st.astype(BF16)) * e_x[:, gs]
        for pair in range(SSD_HPG // 2):
            cols = slice(g * gw + pair * 2 * SSD_P, g * gw + (pair + 1) * 2 * SSD_P)
            xp = xdt[:, cols]
            acc = None
            for half in range(2):
                hh = g * SSD_HPG + pair * 2 + half
                seg = acs[:, hh:hh + 1] - acst[hh:hh + 1, :]
                m = (cb * jnp.where(causal, jnp.exp(jnp.minimum(seg, 0.0)), 0.0)).astype(BF16)
                keep = (lane < SSD_P) if half == 0 else (lane >= SSD_P)
                part = _dot(m, jnp.where(keep, xp, jnp.zeros_like(xp)))
                acc = part if acc is None else acc + part
            y_s[:, cols] += acc
        st_ref[g] = st * e_end[:, gs] + _dot_tn(bg, w_end[:, gs])
    y = y_s[...] + dsk_ref[...] * xs
    zz = z_ref[...].astype(F32)
    y = y * (zz * _sigmoid(zz))
    gnw = gnw_ref[...]
    parts = []
    for g in range(SSD_GROUPS):
        gs = slice(g * gw, (g + 1) * gw)
        parts.append((_rms(y[:, gs]) * gnw[:, gs]).astype(BF16))
    o_ref[...] = x_ref[...] + _dot(jnp.concatenate(parts, axis=-1), wo_ref[...])


def _ssd(x, nw, w_in, conv_w, conv_b, dt_bias, a_log, d_skip, norm_w, w_out, batch, seq_len, tm=512):
    t = x.shape[0]
    steps_per_seq = seq_len // tm
    conv_dim = SSD_DINNER + 2 * SSD_GN
    wb = w_in.astype(BF16)
    wz, wx, wdt = wb[:, :SSD_DINNER], wb[:, SSD_DINNER:SSD_DINNER + conv_dim], wb[:, SSD_DINNER + conv_dim:]
    cw = jnp.concatenate([conv_w, conv_b[None, :], jnp.zeros((3, conv_dim), F32)], axis=0)
    row = pl.BlockSpec((tm, D_MODEL), lambda i: (i, 0))

    def out_row(n):
        return pl.BlockSpec((tm, n), lambda i: (i, 0))

    z, xs, bm, cm, dt, dtt = pl.pallas_call(
        functools.partial(_ssd_proj_kernel, steps_per_seq=steps_per_seq),
        grid=(t // tm,),
        in_specs=[row, _prev_rows_spec(tm, steps_per_seq)] + [_vmem()] * 6,
        out_specs=[out_row(SSD_DINNER), out_row(SSD_DINNER), out_row(SSD_GN), out_row(SSD_GN),
                   out_row(SSD_HEADS), pl.BlockSpec((SSD_HEADS, tm), lambda i: (0, i))],
        out_shape=[jax.ShapeDtypeStruct((t, SSD_DINNER), BF16), jax.ShapeDtypeStruct((t, SSD_DINNER), BF16),
                   jax.ShapeDtypeStruct((t, SSD_GN), BF16), jax.ShapeDtypeStruct((t, SSD_GN), BF16),
                   jax.ShapeDtypeStruct((t, SSD_HEADS), F32), jax.ShapeDtypeStruct((SSD_HEADS, t), F32)],
        scratch_shapes=[pltpu.VMEM((tm + HALO, D_MODEL), BF16)],
        compiler_params=_params("parallel"),
        name="ssd_proj",
    )(x, x, nw.reshape(1, D_MODEL), wz, wx, wdt, wdt.T, cw)

    q = SSD_Q
    spb = seq_len // q
    rexp = jnp.repeat(jnp.eye(SSD_HEADS, dtype=BF16), SSD_P, axis=1)
    dsk = jnp.repeat(d_skip, SSD_P).reshape(1, SSD_DINNER)

    def blk(n):
        return pl.BlockSpec((q, n), lambda b, i: (b * spb + i, 0))

    return pl.pallas_call(
        _ssd_core_kernel,
        grid=(batch, spb),
        in_specs=[blk(D_MODEL), blk(SSD_DINNER), blk(SSD_DINNER), blk(SSD_GN), blk(SSD_GN), blk(SSD_HEADS),
                  pl.BlockSpec((SSD_HEADS, q), lambda b, i: (0, b * spb + i))] + [_vmem()] * 8,
        out_specs=blk(D_MODEL),
        out_shape=jax.ShapeDtypeStruct((t, D_MODEL), F32),
        scratch_shapes=[pltpu.VMEM((SSD_GROUPS, SSD_N, SSD_HPG * SSD_P), F32),
                        pltpu.VMEM((q, SSD_DINNER), F32)],
        compiler_params=_params("parallel", "arbitrary"),
        name="ssd_core",
    )(x, z, xs, bm, cm, dt, dtt, dt_bias.reshape(1, SSD_HEADS), dt_bias.reshape(SSD_HEADS, 1),
      a_log.reshape(1, SSD_HEADS), a_log.reshape(SSD_HEADS, 1), dsk, norm_w.reshape(1, SSD_DINNER), rexp,
      w_out.astype(BF16))


def _sb_proj_kernel(x_ref, nw_ref, wq_ref, wk_ref, wv_ref, q_ref, k_ref, v_ref):
    h = (_rms(x_ref[...]) * nw_ref[...]).astype(BF16)
    q_ref[...] = (_dot(h, wq_ref[...]) * (SB_DH ** -0.5 * LOG2E)).astype(BF16)
    k_ref[...] = _dot(h, wk_ref[...]).astype(BF16)
    v_ref[...] = _dot(h, wv_ref[...]).astype(BF16)


def _sb_attn_kernel(q_ref, k_ref, v_ref, o_ref, acc_ref, z_s, a_s):
    t, nh = SB_T, SB_HPS
    i = pl.program_id(2)
    head_of_lane = lax.broadcasted_iota(jnp.int32, (1, nh * SB_DH), 1) // SB_DH

    def per_head(a):
        return jnp.concatenate([jnp.where(head_of_lane == h, a, jnp.zeros_like(a)) for h in range(nh)], axis=0)

    def key_rows(j):
        return pl.ds(pl.multiple_of(j * t, t), t)

    qs = per_head(q_ref[0])
    upper = _tri(t, lower=True)
    strict = (lax.broadcasted_iota(jnp.int32, (t, t), 0) > lax.broadcasted_iota(jnp.int32, (t, t), 1))

    def logits(j):
        return _dot_nt(qs, k_ref[0, key_rows(j), :])

    def survival(z_of, carry, masked):
        rcs, ys = [], []
        for h in range(nh):
            z = z_of(h)
            sp = _softplus_base2(z)
            if masked:
                sp = jnp.where(strict, sp, 0.0)
            rcs.append(_dot(sp.astype(BF16), upper))
            ys.append(z - carry[h])
        return rcs, ys

    def weights(rcs, ys, masked, heads):
        for h in heads:
            att = jnp.exp2(ys[h] - rcs[h])
            if masked:
                att = jnp.where(strict, att, 0.0)
            a_s[:, h * t:(h + 1) * t] = att.astype(BF16)

    def weighted_values(j):
        return _dot(a_s[...], per_head(v_ref[0, key_rows(j), :]))

    zero = jnp.zeros((t, 1), F32)
    zz = logits(i)
    rcs, ys = survival(lambda h: zz[h * t:(h + 1) * t], (zero,) * nh, True)
    weights(rcs, ys, True, range(nh // 2))
    z_s[...] = logits(jnp.maximum(i - 1, 0))
    weights(rcs, ys, True, range(nh // 2, nh))
    acc_ref[...] = jnp.zeros_like(acc_ref)

    def step(s, carry):
        j = i - 1 - s
        pv = weighted_values(j + 1)
        rcs, ys = survival(lambda h: z_s[h * t:(h + 1) * t, :], carry, False)
        acc_ref[...] += pv
        weights(rcs, ys, False, range(nh // 2))
        z_s[...] = logits(jnp.maximum(j - 1, 0))
        weights(rcs, ys, False, range(nh // 2, nh))
        return tuple(carry[h] + rcs[h][:, 0:1] for h in range(nh))

    lax.fori_loop(0, i, step, tuple(rcs[h][:, 0:1] for h in range(nh)))
    o_ref[0] = (acc_ref[...] + weighted_values(0)).astype(o_ref.dtype)


def _out_proj_kernel(x_ref, a_ref, w_ref, o_ref):
    o_ref[...] = x_ref[...] + _dot(a_ref[...], w_ref[...])


def _sb(x, nw, w_qkv, w_out, batch, seq_len, tm=512):
    t = x.shape[0]
    wb = w_qkv.astype(BF16)
    row = pl.BlockSpec((tm, D_MODEL), lambda i: (i, 0))
    qkv_shape = jax.ShapeDtypeStruct((t, D_MODEL), BF16)
    q, k, v = pl.pallas_call(
        _sb_proj_kernel,
        grid=(t // tm,),
        in_specs=[row] + [_vmem()] * 4,
        out_specs=[row, row, row],
        out_shape=[qkv_shape, qkv_shape, qkv_shape],
        compiler_params=_params("parallel"),
        name="sb_proj",
    )(x, nw.reshape(1, D_MODEL), wb[:, :D_MODEL], wb[:, D_MODEL:2 * D_MODEL], wb[:, 2 * D_MODEL:])

    shape3 = (batch, seq_len, D_MODEL)
    lanes = SB_HPS * SB_DH
    qblk = pl.BlockSpec((1, SB_T, lanes), lambda b, p, i: (b, i, p))
    kvblk = pl.BlockSpec((1, seq_len, lanes), lambda b, p, i: (b, 0, p))
    att = pl.pallas_call(
        _sb_attn_kernel,
        grid=(batch, SB_HEADS // SB_HPS, seq_len // SB_T),
        in_specs=[qblk, kvblk, kvblk],
        out_specs=qblk,
        out_shape=jax.ShapeDtypeStruct(shape3, BF16),
        scratch_shapes=[pltpu.VMEM((SB_T, lanes), F32), pltpu.VMEM((SB_HPS * SB_T, SB_T), F32),
                        pltpu.VMEM((SB_T, SB_HPS * SB_T), BF16)],
        compiler_params=_params("parallel", "parallel", "arbitrary"),
        name="sb_attention",
    )(q.reshape(shape3), k.reshape(shape3), v.reshape(shape3))

    return pl.pallas_call(
        _out_proj_kernel,
        grid=(t // tm,),
        in_specs=[row, row, _vmem()],
        out_specs=row,
        out_shape=jax.ShapeDtypeStruct((t, D_MODEL), F32),
        compiler_params=_params("parallel"),
        name="sb_out_proj",
    )(x, att.reshape(t, D_MODEL), w_out.astype(BF16))


def kernel(x, mix_norm_w, ffn_norm_w, final_norm_w, gla_w_in, gla_w_gate2, gla_b_gate, gla_norm_w, gla_w_out, pool_w, pool_b, pool_scale, ssd_w_in, ssd_conv_w, ssd_conv_b, ssd_dt_bias, ssd_a_log, ssd_d, ssd_norm_w, ssd_w_out, sb_w_qkv, sb_w_out, ffn_w_up, ffn_conv_w, ffn_conv_b, ffn_w_down):
    batch, seq_len, d = x.shape
    assert d == D_MODEL
    depth = mix_norm_w.shape[0]
    xf = x.reshape(batch * seq_len, d)
    for i in range(depth):
        m, j = i % 4, i // 4
        if m == 0:
            xf = _gla(xf, mix_norm_w[i], gla_w_in[j], gla_w_gate2[j], gla_b_gate[j], gla_norm_w[j], gla_w_out[j],
                      batch, seq_len)
        elif m == 1:
            xf = _pool(xf, mix_norm_w[i], pool_w[j], pool_b[j], pool_scale[j], seq_len)
        elif m == 2:
            xf = _ssd(xf, mix_norm_w[i], ssd_w_in[j], ssd_conv_w[j], ssd_conv_b[j], ssd_dt_bias[j], ssd_a_log[j],
                      ssd_d[j], ssd_norm_w[j], ssd_w_out[j], batch, seq_len)
        else:
            xf = _sb(xf, mix_norm_w[i], sb_w_qkv[j], sb_w_out[j], batch, seq_len)
        xf = _ffn(xf, ffn_norm_w[i], ffn_w_up[i], ffn_conv_w[i], ffn_conv_b[i], ffn_w_down[i], final_norm_w,
                  seq_len, final_norm=(i == depth - 1))
    return xf.reshape(batch, seq_len, d)
```

```python
import functools

import jax
import jax.numpy as jnp
from jax import lax
from jax.experimental import pallas as pl
from jax.experimental.pallas import tpu as pltpu

F32 = jnp.float32
BF16 = jnp.bfloat16

EPS = 1e-6
LOG2E = 1.4426950408889634
D_MODEL = 1024
HALO = 16
VMEM_LIMIT = 56 * 1024 * 1024

GLA_HEADS, GLA_DK, GLA_DV, GLA_RANK, GLA_TAU, GLA_CHUNK = 4, 128, 256, 16, 16.0, 64
GLA_HK, GLA_HV = GLA_HEADS * GLA_DK, GLA_HEADS * GLA_DV
POOL_WINDOWS, POOL_GW = (2, 4, 8, 16), 256
SSD_DINNER, SSD_P, SSD_HEADS, SSD_GROUPS, SSD_HPG, SSD_N, SSD_CONV = 2048, 64, 32, 4, 8, 128, 4
SSD_GN = SSD_GROUPS * SSD_N
SSD_Q = 128
SSD_PROJ_TILE = 256
SB_HEADS, SB_DH, SB_T = 16, 64, 256
SB_HPS = 4
SB_SLABS = 1
FFN_DIM, FFN_TILE, FFN_CONV = 2816, 256, 3
FFN_NT = FFN_DIM // FFN_TILE


def _params(*sem):
    return pltpu.CompilerParams(dimension_semantics=sem, vmem_limit_bytes=VMEM_LIMIT)


def _vmem():
    return pl.BlockSpec(memory_space=pltpu.VMEM)


def _rms(x):
    return x * lax.rsqrt(jnp.mean(x * x, axis=-1, keepdims=True) + EPS)


def _softplus(x):
    return jnp.maximum(x, 0.0) + jnp.log(1.0 + jnp.exp(-jnp.abs(x)))


def _softplus_base2(x):
    return jnp.where(x > 64.0, x, jnp.log(1.0 + jnp.exp2(x)) * LOG2E)


def _sigmoid(x):
    return 1.0 / (1.0 + jnp.exp(-x))


def _dot(a, b):
    return jnp.dot(a, b, preferred_element_type=F32)


def _dot_nt(a, b):
    return lax.dot_general(a, b, (((1,), (1,)), ((), ())), preferred_element_type=F32)


def _dot_tn(a, b):
    return lax.dot_general(a, b, (((0,), (0,)), ((), ())), preferred_element_type=F32)


def _split3(x):
    hi = x.astype(BF16)
    r = x - hi.astype(F32)
    mid = r.astype(BF16)
    lo = (r - mid.astype(F32)).astype(BF16)
    return hi, mid, lo


def _exact_left(m01, x):
    hi, mid, lo = _split3(x)
    return _dot(m01, lo) + _dot(m01, mid) + _dot(m01, hi)


def _exact_right(x, m01):
    hi, mid, lo = _split3(x)
    return _dot(lo, m01) + _dot(mid, m01) + _dot(hi, m01)


def _tri(n, lower):
    r = lax.broadcasted_iota(jnp.int32, (n, n), 0)
    c = lax.broadcasted_iota(jnp.int32, (n, n), 1)
    return jnp.where((r >= c) if lower else (r <= c), 1.0, 0.0).astype(BF16)


def _prev_rows_spec(tm, steps_per_seq):
    del steps_per_seq
    return pl.BlockSpec((HALO, D_MODEL), lambda i: (jnp.maximum(i * (tm // HALO) - 1, 0), 0))


def _normed_with_halo(x_ref, xp_ref, nw_ref, hs_ref, steps_per_seq):
    first = (pl.program_id(0) % steps_per_seq) == 0
    nw = nw_ref[...]
    hp = _rms(xp_ref[...]) * nw
    hs_ref[0:HALO, :] = jnp.where(first, 0.0, hp).astype(hs_ref.dtype)
    hs_ref[HALO:, :] = (_rms(x_ref[...]) * nw).astype(hs_ref.dtype)


def _ffn_kernel(x_ref, xp_ref, nw_ref, wu_ref, cw_ref, wd_ref, fw_ref, o_ref, hs_ref, a_ref,
                *, steps_per_seq, final_norm):
    _normed_with_halo(x_ref, xp_ref, nw_ref, hs_ref, steps_per_seq)

    def up(j):
        hs = hs_ref[...]
        lo = j * FFN_TILE
        return (_dot(hs, wu_ref[:, lo:lo + FFN_TILE]),
                _dot(hs, wu_ref[:, FFN_DIM + lo:FFN_DIM + lo + FFN_TILE]))

    def conv(u, lo):
        c = cw_ref[:, lo:lo + FFN_TILE]
        out = u[HALO:] * c[2:3] + c[3:4]
        out = out + pltpu.roll(u, 1, 0)[HALO:] * c[1:2]
        out = out + pltpu.roll(u, 2, 0)[HALO:] * c[0:1]
        return out

    u = up(0)
    for j in range(FFN_NT):
        nxt = up(j + 1) if j + 1 < FFN_NT else None
        g = conv(u[0], j * FFN_TILE)
        v = conv(u[1], FFN_DIM + j * FFN_TILE)
        a_ref[:, j * FFN_TILE:(j + 1) * FFN_TILE] = (g * _sigmoid(g) * v).astype(BF16)
        u = nxt
    y = x_ref[...] + _dot(a_ref[...], wd_ref[...])
    if final_norm:
        y = _rms(y) * fw_ref[...]
    o_ref[...] = y


def _ffn(x, nw, w_up, conv_w, conv_b, w_down, final_w, seq_len, final_norm, tm=512):
    t = x.shape[0]
    steps_per_seq = seq_len // tm
    cw = jnp.concatenate([conv_w, conv_b[None, :], jnp.zeros((4, 2 * FFN_DIM), F32)], axis=0)
    row = pl.BlockSpec((tm, D_MODEL), lambda i: (i, 0))
    return pl.pallas_call(
        functools.partial(_ffn_kernel, steps_per_seq=steps_per_seq, final_norm=final_norm),
        grid=(t // tm,),
        in_specs=[row, _prev_rows_spec(tm, steps_per_seq)] + [_vmem()] * 5,
        out_specs=row,
        out_shape=jax.ShapeDtypeStruct((t, D_MODEL), F32),
        scratch_shapes=[pltpu.VMEM((tm + HALO, D_MODEL), BF16), pltpu.VMEM((tm, FFN_DIM), BF16)],
        compiler_params=_params("parallel"),
        name="conv_ffn",
    )(x, x, nw.reshape(1, D_MODEL), w_up.astype(BF16), cw, w_down.astype(BF16), final_w.reshape(1, D_MODEL))


def _gla_kernel(x_ref, nw_ref, wq_ref, wk_ref, wv_ref, wr_ref, wg1_ref, wg2_ref, bg_ref, gnw_ref, wo_ref,
                o_ref, st_ref, q_s, k_s, v_s, la_s, o_s):
    tm = x_ref.shape[0]

    @pl.when(pl.program_id(1) == 0)
    def _():
        st_ref[...] = jnp.zeros_like(st_ref)

    x = x_ref[...]
    h = (_rms(x) * nw_ref[...]).astype(BF16)
    q_s[...] = _dot(h, wq_ref[...]).astype(BF16)
    k_s[...] = _dot(h, wk_ref[...])
    v_s[...] = _dot(h, wv_ref[...]).astype(BF16)
    glr = _dot(h, wg1_ref[...]).astype(BF16)
    gate = _dot(glr, wg2_ref[...]) + bg_ref[...]
    la_s[...] = -_softplus(-gate) * (1.0 / GLA_TAU)

    tril = _tri(GLA_CHUNK, lower=True)
    scale = GLA_DK ** -0.5
    nchunks = tm // GLA_CHUNK
    k_cols = [slice(hd * GLA_DK, (hd + 1) * GLA_DK) for hd in range(GLA_HEADS)]
    v_cols = [slice(hd * GLA_DV, (hd + 1) * GLA_DV) for hd in range(GLA_HEADS)]

    def chunk_update(c):
        rows = slice(c * GLA_CHUNK, (c + 1) * GLA_CHUNK)
        g = _exact_left(tril, la_s[rows, :])
        g_end = g[GLA_CHUNK - 1:GLA_CHUNK, :]
        kd = (k_s[rows, :] * jnp.exp(g_end - g)).astype(BF16)
        vc = v_s[rows, :]
        return [_dot_tn(vc[:, v_cols[hd]], kd[:, k_cols[hd]]) for hd in range(GLA_HEADS)], jnp.exp(g_end)

    state = [st_ref[hd] for hd in range(GLA_HEADS)]
    pending = chunk_update(0)
    for c in range(nchunks):
        upd, eg = pending
        if c + 1 < nchunks:
            pending = chunk_update(c + 1)
        rows = slice(c * GLA_CHUNK, (c + 1) * GLA_CHUNK)
        qc = q_s[rows, :]
        for hd in range(GLA_HEADS):
            state[hd] = state[hd] * eg[:, k_cols[hd]] + upd[hd]
            o_s[rows, v_cols[hd]] = _dot_nt(qc[:, k_cols[hd]], state[hd].astype(BF16)) * scale
    for hd in range(GLA_HEADS):
        st_ref[hd] = state[hd]

    r = _dot(h, wr_ref[...])
    gated = r * _sigmoid(r)
    gnw = gnw_ref[...]
    parts = []
    for hd in range(GLA_HEADS):
        vs = slice(hd * GLA_DV, (hd + 1) * GLA_DV)
        parts.append((_rms(o_s[:, vs]) * gnw[:, vs] * gated[:, vs]).astype(BF16))
    o_ref[...] = x + _dot(jnp.concatenate(parts, axis=-1), wo_ref[...])


def _gla(x, nw, w_in, w_gate2, b_gate, norm_w, w_out, batch, seq_len, tm=512):
    t = x.shape[0]
    spb = seq_len // tm
    wb = w_in.astype(BF16)
    wq, wk = wb[:, :GLA_HK], wb[:, GLA_HK:2 * GLA_HK]
    wv, wr = wb[:, 2 * GLA_HK:2 * GLA_HK + GLA_HV], wb[:, 2 * GLA_HK + GLA_HV:2 * GLA_HK + 2 * GLA_HV]
    wg1 = jnp.pad(wb[:, 2 * GLA_HK + 2 * GLA_HV:], ((0, 0), (0, 128 - GLA_RANK)))
    wg2 = jnp.pad(w_gate2.astype(BF16), ((0, 128 - GLA_RANK), (0, 0)))
    row = pl.BlockSpec((tm, D_MODEL), lambda b, i: (b * spb + i, 0))
    return pl.pallas_call(
        _gla_kernel,
        grid=(batch, spb),
        in_specs=[row] + [_vmem()] * 10,
        out_specs=row,
        out_shape=jax.ShapeDtypeStruct((t, D_MODEL), F32),
        scratch_shapes=[pltpu.VMEM((GLA_HEADS, GLA_DV, GLA_DK), F32),
                        pltpu.VMEM((tm, GLA_HK), BF16), pltpu.VMEM((tm, GLA_HK), F32),
                        pltpu.VMEM((tm, GLA_HV), BF16), pltpu.VMEM((tm, GLA_HK), F32),
                        pltpu.VMEM((tm, GLA_HV), F32)],
        compiler_params=_params("parallel", "arbitrary"),
        name="gla_mixer",
    )(x, nw.reshape(1, D_MODEL), wq, wk, wv, wr, wg1, wg2, b_gate.reshape(1, GLA_HK),
      norm_w.reshape(1, GLA_HV), w_out.astype(BF16))


def _pool_kernel(x_ref, xp_ref, nw_ref, w_ref, b_ref, sc_ref, o_ref, hs_ref, *, steps_per_seq):
    tm = x_ref.shape[0]
    _normed_with_halo(x_ref, xp_ref, nw_ref, hs_ref, steps_per_seq)
    pos = (pl.program_id(0) % steps_per_seq) * tm + lax.broadcasted_iota(jnp.int32, (tm, 1), 0)
    outs = []
    for gi, win in enumerate(POOL_WINDOWS):
        cols = slice(gi * POOL_GW, (gi + 1) * POOL_GW)
        cur = hs_ref[HALO:, cols]
        total = cur
        for back in range(1, win):
            total = total + hs_ref[HALO - back:HALO - back + tm, cols]
        count = jnp.minimum(pos + 1, win).astype(F32)
        dlt = (total / count - cur).astype(BF16)
        outs.append(_dot(dlt, w_ref[gi]) + b_ref[gi])
    o_ref[...] = x_ref[...] + jnp.concatenate(outs, axis=-1) * sc_ref[...]


def _pool(x, nw, w_grp, b_grp, scale, seq_len, tm=512):
    t = x.shape[0]
    steps_per_seq = seq_len // tm
    row = pl.BlockSpec((tm, D_MODEL), lambda i: (i, 0))
    return pl.pallas_call(
        functools.partial(_pool_kernel, steps_per_seq=steps_per_seq),
        grid=(t // tm,),
        in_specs=[row, _prev_rows_spec(tm, steps_per_seq), _vmem(), _vmem(), _vmem(), _vmem()],
        out_specs=row,
        out_shape=jax.ShapeDtypeStruct((t, D_MODEL), F32),
        scratch_shapes=[pltpu.VMEM((tm + HALO, D_MODEL), F32)],
        compiler_params=_params("parallel"),
        name="pool_mixer",
    )(x, x, nw.reshape(1, D_MODEL), w_grp.astype(BF16), b_grp.reshape(len(POOL_WINDOWS), 1, POOL_GW),
      scale.reshape(1, D_MODEL))


def _ssd_proj_kernel(x_ref, xp_ref, nw_ref, wz_ref, wx_ref, wdt_ref, wdtt_ref, cw_ref, z_ref, xs_ref, bm_ref,
                     cm_ref, dt_ref, dtt_ref, hs_ref, u0_ref, u1_ref, *, steps_per_seq):
    tm = x_ref.shape[0]
    _normed_with_halo(x_ref, xp_ref, nw_ref, hs_ref, steps_per_seq)
    h = hs_ref[HALO:, :]
    tile, ztile = SSD_PROJ_TILE, 2 * SSD_PROJ_TILE
    ntiles = (SSD_DINNER + 2 * SSD_GN) // tile
    bufs = (u0_ref, u1_ref)

    def up(j):
        bufs[j % 2][...] = _dot(hs_ref[...], wx_ref[:, j * tile:(j + 1) * tile])

    def conv(j):
        lo = j * tile
        c = cw_ref[:, lo:lo + tile]
        if lo < SSD_DINNER:
            dst, off = xs_ref, lo
        elif lo < SSD_DINNER + SSD_GN:
            dst, off = bm_ref, lo - SSD_DINNER
        else:
            dst, off = cm_ref, lo - SSD_DINNER - SSD_GN
        src = bufs[j % 2]
        acc = src[HALO:, :] * c[3:4] + c[4:5]
        for back in range(1, SSD_CONV):
            acc = acc + src[HALO - back:HALO - back + tm, :] * c[3 - back:4 - back]
        dst[:, off:off + tile] = (acc * _sigmoid(acc)).astype(BF16)

    up(0)
    for j in range(ntiles):
        if j + 1 < ntiles:
            up(j + 1)
        if j % 3 == 2:
            zc = slice((j // 3) * ztile, (j // 3 + 1) * ztile)
            z_ref[:, zc] = _dot(h, wz_ref[:, zc]).astype(z_ref.dtype)
        conv(j)
    dt_ref[...] = _dot(h, wdt_ref[...])
    dtt_ref[...] = _dot_nt(wdtt_ref[...], h)


def _ssd_core_kernel(x_ref, z_ref, xs_ref, bm_ref, cm_ref, dt_ref, dtt_ref, dtb_r, dtb_c, alog_r, alog_c,
                     dsk_ref, gnw_ref, rexp_ref, wo_ref, o_ref, st_ref, y_s, ex_s, we_s):
    tm, q = x_ref.shape[0], SSD_Q
    gw = SSD_HPG * SSD_P

    @pl.when(pl.program_id(1) == 0)
    def _():
        st_ref[...] = jnp.zeros_like(st_ref)

    dt = _softplus(dt_ref[...] + dtb_r[...])
    dtt = _softplus(dtt_ref[...] + dtb_c[...])
    a = dt * -jnp.exp(alog_r[...])
    at = dtt * -jnp.exp(alog_c[...])
    tri_l, tri_u = _tri(q, lower=True), _tri(q, lower=False)
    acs, acst, per_row = [], [], []
    for b in range(tm // q):
        rows = slice(b * q, (b + 1) * q)
        acs_b = _exact_left(tri_l, a[rows])
        acs.append(acs_b)
        acst.append(_exact_right(at[:, rows], tri_u))
        per_row.append(jnp.concatenate([jnp.exp(acs_b), dt[rows] * jnp.exp(acs_b[q - 1:q] - acs_b)], axis=1))
    ew = jnp.concatenate(per_row, axis=0)
    ew_hi = ew.astype(BF16)
    ew_parts = jnp.concatenate([ew_hi, (ew - ew_hi.astype(F32)).astype(BF16)], axis=1)
    ex_s[...] = _dot(ew_parts, rexp_ref[:, :SSD_DINNER])
    we_s[...] = (xs_ref[...].astype(F32) * _dot(ew_parts, rexp_ref[:, SSD_DINNER:])).astype(BF16)

    causal = (lax.broadcasted_iota(jnp.int32, (q, q), 0) >= lax.broadcasted_iota(jnp.int32, (q, q), 1))
    low_half = lax.broadcasted_iota(jnp.int32, (1, 2 * SSD_P), 1) < SSD_P
    state = [st_ref[g] for g in range(SSD_GROUPS)]
    for b in range(tm // q):
        rows = slice(b * q, (b + 1) * q)
        for g in range(SSD_GROUPS):
            ns = slice(g * SSD_N, (g + 1) * SSD_N)
            gs = slice(g * gw, (g + 1) * gw)
            bg = bm_ref[rows, ns]
            cg = cm_ref[rows, ns]
            cb = _dot_nt(cg, bg)
            update = _dot_tn(bg, we_s[rows, gs])
            pieces = []
            for pair in range(SSD_HPG // 2):
                cols = slice(g * gw + pair * 2 * SSD_P, g * gw + (pair + 1) * 2 * SSD_P)
                xp = xs_ref[rows, cols]
                ms = []
                for half in range(2):
                    hh = g * SSD_HPG + pair * 2 + half
                    seg = acs[b][:, hh:hh + 1] - acst[b][hh:hh + 1, :]
                    decay = jnp.exp(jnp.where(causal, seg, -1e30))
                    ms.append((cb * decay * dtt[hh:hh + 1, rows]).astype(BF16))
                xp2 = jnp.concatenate([jnp.where(low_half, xp, jnp.zeros_like(xp)),
                                       jnp.where(low_half, jnp.zeros_like(xp), xp)], axis=0)
                pieces.append(_dot(jnp.concatenate(ms, axis=1), xp2))
            y_s[rows, gs] = (_dot(cg, state[g].astype(BF16)) * ex_s[rows, gs]
                             + jnp.concatenate(pieces, axis=1))
            state[g] = state[g] * ex_s[(b + 1) * q - 1:(b + 1) * q, gs] + update
    for g in range(SSD_GROUPS):
        st_ref[g] = state[g]

    gnw = gnw_ref[...]
    dsk = dsk_ref[...]
    parts = []
    for g in range(SSD_GROUPS):
        gs = slice(g * gw, (g + 1) * gw)
        zz = z_ref[:, gs].astype(F32)
        y = (y_s[:, gs] + dsk[:, gs] * xs_ref[:, gs].astype(F32)) * (zz * _sigmoid(zz))
        parts.append((_rms(y) * gnw[:, gs]).astype(BF16))
    o_ref[...] = x_ref[...] + _dot(jnp.concatenate(parts, axis=-1), wo_ref[...])


def _ssd(x, nw, w_in, conv_w, conv_b, dt_bias, a_log, d_skip, norm_w, w_out, batch, seq_len, tm=512):
    t = x.shape[0]
    steps_per_seq = seq_len // tm
    conv_dim = SSD_DINNER + 2 * SSD_GN
    wb = w_in.astype(BF16)
    wz, wx, wdt = wb[:, :SSD_DINNER], wb[:, SSD_DINNER:SSD_DINNER + conv_dim], wb[:, SSD_DINNER + conv_dim:]
    cw = jnp.concatenate([conv_w, conv_b[None, :], jnp.zeros((3, conv_dim), F32)], axis=0)
    row = pl.BlockSpec((tm, D_MODEL), lambda i: (i, 0))

    def out_row(n):
        return pl.BlockSpec((tm, n), lambda i: (i, 0))

    z, xs, bm, cm, dt, dtt = pl.pallas_call(
        functools.partial(_ssd_proj_kernel, steps_per_seq=steps_per_seq),
        grid=(t // tm,),
        in_specs=[row, _prev_rows_spec(tm, steps_per_seq)] + [_vmem()] * 6,
        out_specs=[out_row(SSD_DINNER), out_row(SSD_DINNER), out_row(SSD_GN), out_row(SSD_GN),
                   out_row(SSD_HEADS), pl.BlockSpec((SSD_HEADS, tm), lambda i: (0, i))],
        out_shape=[jax.ShapeDtypeStruct((t, SSD_DINNER), BF16), jax.ShapeDtypeStruct((t, SSD_DINNER), BF16),
                   jax.ShapeDtypeStruct((t, SSD_GN), BF16), jax.ShapeDtypeStruct((t, SSD_GN), BF16),
                   jax.ShapeDtypeStruct((t, SSD_HEADS), F32), jax.ShapeDtypeStruct((SSD_HEADS, t), F32)],
        scratch_shapes=[pltpu.VMEM((tm + HALO, D_MODEL), BF16), pltpu.VMEM((tm + HALO, SSD_PROJ_TILE), F32),
                        pltpu.VMEM((tm + HALO, SSD_PROJ_TILE), F32)],
        compiler_params=_params("parallel"),
        name="ssd_proj",
    )(x, x, nw.reshape(1, D_MODEL), wz, wx, wdt, wdt.T, cw)

    spb = seq_len // tm
    spread = jnp.repeat(jnp.eye(SSD_HEADS, dtype=BF16), SSD_P, axis=1)
    zero = jnp.zeros_like(spread)
    top = jnp.concatenate([jnp.concatenate([spread, zero], axis=1), jnp.concatenate([zero, spread], axis=1)], axis=0)
    rexp = jnp.concatenate([top, top], axis=0)
    dsk = jnp.repeat(d_skip, SSD_P).reshape(1, SSD_DINNER)

    def blk(n):
        return pl.BlockSpec((tm, n), lambda b, i: (b * spb + i, 0))

    return pl.pallas_call(
        _ssd_core_kernel,
        grid=(batch, spb),
        in_specs=[blk(D_MODEL), blk(SSD_DINNER), blk(SSD_DINNER), blk(SSD_GN), blk(SSD_GN), blk(SSD_HEADS),
                  pl.BlockSpec((SSD_HEADS, tm), lambda b, i: (0, b * spb + i))] + [_vmem()] * 8,
        out_specs=blk(D_MODEL),
        out_shape=jax.ShapeDtypeStruct((t, D_MODEL), F32),
        scratch_shapes=[pltpu.VMEM((SSD_GROUPS, SSD_N, SSD_HPG * SSD_P), F32),
                        pltpu.VMEM((tm, SSD_DINNER), F32), pltpu.VMEM((tm, SSD_DINNER), F32),
                        pltpu.VMEM((tm, SSD_DINNER), BF16)],
        compiler_params=_params("parallel", "arbitrary"),
        name="ssd_core",
    )(x, z, xs, bm, cm, dt, dtt, dt_bias.reshape(1, SSD_HEADS), dt_bias.reshape(SSD_HEADS, 1),
      a_log.reshape(1, SSD_HEADS), a_log.reshape(SSD_HEADS, 1), dsk, norm_w.reshape(1, SSD_DINNER), rexp,
      w_out.astype(BF16))


def _sb_proj_kernel(x_ref, nw_ref, wq_ref, wk_ref, wv_ref, q_ref, k_ref, v_ref):
    h = (_rms(x_ref[...]) * nw_ref[...]).astype(BF16)
    q_ref[...] = (_dot(h, wq_ref[...]) * (SB_DH ** -0.5 * LOG2E)).astype(BF16)
    k_ref[...] = _dot(h, wk_ref[...]).astype(BF16)
    v_ref[...] = _dot(h, wv_ref[...]).astype(BF16)


def _sb_attn_kernel(q_ref, k_ref, v_ref, o_ref, acc_ref, z_s, a_s):
    t, nh, ns = SB_T, SB_HPS, SB_SLABS
    w = nh * SB_DH
    nc = ns * nh
    i = pl.program_id(2)
    head_of_lane = lax.broadcasted_iota(jnp.int32, (1, w), 1) // SB_DH

    def per_head(a):
        return jnp.concatenate([jnp.where(head_of_lane == h, a, jnp.zeros_like(a)) for h in range(nh)], axis=0)

    def key_rows(j):
        return pl.ds(pl.multiple_of(j * t, t), t)

    qs = [per_head(q_ref[0, :, sl * w:(sl + 1) * w]) for sl in range(ns)]
    upper = _tri(t, lower=True)
    strict = (lax.broadcasted_iota(jnp.int32, (t, t), 0) > lax.broadcasted_iota(jnp.int32, (t, t), 1))

    def logits(j, sl):
        return _dot_nt(qs[sl], k_ref[0, key_rows(j), sl * w:(sl + 1) * w])

    def survival(z_of, carry, masked):
        rcs, ys = [], []
        for c in range(nc):
            z = z_of(c)
            sp = _softplus_base2(z)
            if masked:
                sp = jnp.where(strict, sp, 0.0)
            rcs.append(_dot(sp.astype(BF16), upper))
            ys.append(z - carry[c])
        return rcs, ys

    def weights(rcs, ys, masked, chains):
        for c in chains:
            att = jnp.exp2(ys[c] - rcs[c])
            if masked:
                att = jnp.where(strict, att, 0.0)
            a_s[:, c * t:(c + 1) * t] = att.astype(BF16)

    def weighted_values(j):
        return [_dot(a_s[:, sl * nh * t:(sl + 1) * nh * t], per_head(v_ref[0, key_rows(j), sl * w:(sl + 1) * w]))
                for sl in range(ns)]

    def next_logits(j):
        for sl in range(ns):
            z_s[sl * nh * t:(sl + 1) * nh * t, :] = logits(jnp.maximum(j, 0), sl)

    zero = jnp.zeros((t, 1), F32)
    zz = [logits(i, sl) for sl in range(ns)]
    rcs, ys = survival(lambda c: zz[c // nh][(c % nh) * t:(c % nh + 1) * t], (zero,) * nc, True)
    weights(rcs, ys, True, range(nc // 2))
    next_logits(i - 1)
    weights(rcs, ys, True, range(nc // 2, nc))
    acc_ref[...] = jnp.zeros_like(acc_ref)

    def step(s, carry):
        j = i - 1 - s
        pv = weighted_values(j + 1)
        rcs, ys = survival(lambda c: z_s[c * t:(c + 1) * t, :], carry, False)
        for sl in range(ns):
            acc_ref[:, sl * w:(sl + 1) * w] += pv[sl]
        weights(rcs, ys, False, range(nc // 2))
        next_logits(j - 1)
        weights(rcs, ys, False, range(nc // 2, nc))
        return tuple(carry[c] + rcs[c][:, 0:1] for c in range(nc))

    lax.fori_loop(0, i, step, tuple(rcs[c][:, 0:1] for c in range(nc)))
    pv = weighted_values(0)
    for sl in range(ns):
        o_ref[0, :, sl * w:(sl + 1) * w] = (acc_ref[:, sl * w:(sl + 1) * w] + pv[sl]).astype(o_ref.dtype)


def _out_proj_kernel(x_ref, a_ref, w_ref, o_ref):
    o_ref[...] = x_ref[...] + _dot(a_ref[...], w_ref[...])


def _sb(x, nw, w_qkv, w_out, batch, seq_len, tm=512):
    t = x.shape[0]
    wb = w_qkv.astype(BF16)
    row = pl.BlockSpec((tm, D_MODEL), lambda i: (i, 0))
    qkv_shape = jax.ShapeDtypeStruct((t, D_MODEL), BF16)
    q, k, v = pl.pallas_call(
        _sb_proj_kernel,
        grid=(t // tm,),
        in_specs=[row] + [_vmem()] * 4,
        out_specs=[row, row, row],
        out_shape=[qkv_shape, qkv_shape, qkv_shape],
        compiler_params=_params("parallel"),
        name="sb_proj",
    )(x, nw.reshape(1, D_MODEL), wb[:, :D_MODEL], wb[:, D_MODEL:2 * D_MODEL], wb[:, 2 * D_MODEL:])

    shape3 = (batch, seq_len, D_MODEL)
    lanes = SB_SLABS * SB_HPS * SB_DH
    chains = SB_SLABS * SB_HPS
    qblk = pl.BlockSpec((1, SB_T, lanes), lambda b, p, i: (b, i, p))
    kvblk = pl.BlockSpec((1, seq_len, lanes), lambda b, p, i: (b, 0, p))
    att = pl.pallas_call(
        _sb_attn_kernel,
        grid=(batch, SB_HEADS // chains, seq_len // SB_T),
        in_specs=[qblk, kvblk, kvblk],
        out_specs=qblk,
        out_shape=jax.ShapeDtypeStruct(shape3, BF16),
        scratch_shapes=[pltpu.VMEM((SB_T, lanes), F32), pltpu.VMEM((chains * SB_T, SB_T), F32),
                        pltpu.VMEM((SB_T, chains * SB_T), BF16)],
        compiler_params=_params("parallel", "parallel", "arbitrary"),
        name="sb_attention",
    )(q.reshape(shape3), k.reshape(shape3), v.reshape(shape3))

    return pl.pallas_call(
        _out_proj_kernel,
        grid=(t // tm,),
        in_specs=[row, row, _vmem()],
        out_specs=row,
        out_shape=jax.ShapeDtypeStruct((t, D_MODEL), F32),
        compiler_params=_params("parallel"),
        name="sb_out_proj",
    )(x, att.reshape(t, D_MODEL), w_out.astype(BF16))


def kernel(x, mix_norm_w, ffn_norm_w, final_norm_w, gla_w_in, gla_w_gate2, gla_b_gate, gla_norm_w, gla_w_out, pool_w, pool_b, pool_scale, ssd_w_in, ssd_conv_w, ssd_conv_b, ssd_dt_bias, ssd_a_log, ssd_d, ssd_norm_w, ssd_w_out, sb_w_qkv, sb_w_out, ffn_w_up, ffn_conv_w, ffn_conv_b, ffn_w_down):
    batch, seq_len, d = x.shape
    assert d == D_MODEL
    depth = mix_norm_w.shape[0]
    xf = x.reshape(batch * seq_len, d)
    for i in range(depth):
        m, j = i % 4, i // 4
        if m == 0:
            xf = _gla(xf, mix_norm_w[i], gla_w_in[j], gla_w_gate2[j], gla_b_gate[j], gla_norm_w[j], gla_w_out[j],
                      batch, seq_len)
        elif m == 1:
            xf = _pool(xf, mix_norm_w[i], pool_w[j], pool_b[j], pool_scale[j], seq_len)
        elif m == 2:
            xf = _ssd(xf, mix_norm_w[i], ssd_w_in[j], ssd_conv_w[j], ssd_conv_b[j], ssd_dt_bias[j], ssd_a_log[j],
                      ssd_d[j], ssd_norm_w[j], ssd_w_out[j], batch, seq_len)
        else:
            xf = _sb(xf, mix_norm_w[i], sb_w_qkv[j], sb_w_out[j], batch, seq_len)
        xf = _ffn(xf, ffn_norm_w[i], ffn_w_up[i], ffn_conv_w[i], ffn_conv_b[i], ffn_w_down[i], final_norm_w,
                  seq_len, final_norm=(i == depth - 1))
    return xf.reshape(batch, seq_len, d)
```

```python
import functools

import jax
import jax.numpy as jnp
from jax import lax
from jax.experimental import pallas as pl
from jax.experimental.pallas import tpu as pltpu

F32 = jnp.float32
BF16 = jnp.bfloat16

EPS = 1e-6
LOG2E = 1.4426950408889634
D_MODEL = 1024
HALO = 16
VMEM_LIMIT = 56 * 1024 * 1024

GLA_HEADS, GLA_DK, GLA_DV, GLA_RANK, GLA_TAU, GLA_CHUNK = 4, 128, 256, 16, 16.0, 64
GLA_HK, GLA_HV = GLA_HEADS * GLA_DK, GLA_HEADS * GLA_DV
POOL_WINDOWS, POOL_GW = (2, 4, 8, 16), 256
SSD_DINNER, SSD_P, SSD_HEADS, SSD_GROUPS, SSD_HPG, SSD_N, SSD_CONV = 2048, 64, 32, 4, 8, 128, 4
SSD_GN = SSD_GROUPS * SSD_N
SSD_Q = 128
SSD_PROJ_TILE = 256
SB_HEADS, SB_DH, SB_T = 16, 64, 256
SB_HPS = 4
SB_SLABS = 1
FFN_DIM, FFN_TILE, FFN_CONV = 2816, 256, 3
FFN_NT = FFN_DIM // FFN_TILE


def _params(*sem):
    return pltpu.CompilerParams(dimension_semantics=sem, vmem_limit_bytes=VMEM_LIMIT)


def _vmem():
    return pl.BlockSpec(memory_space=pltpu.VMEM)


def _rms(x):
    return x * lax.rsqrt(jnp.mean(x * x, axis=-1, keepdims=True) + EPS)


def _softplus(x):
    return jnp.maximum(x, 0.0) + jnp.log(1.0 + jnp.exp(-jnp.abs(x)))


def _softplus_base2(x):
    return jnp.where(x > 64.0, x, jnp.log(1.0 + jnp.exp2(x)) * LOG2E)


def _sigmoid(x):
    return 1.0 / (1.0 + jnp.exp(-x))


def _dot(a, b):
    return jnp.dot(a, b, preferred_element_type=F32)


def _dot_nt(a, b):
    return lax.dot_general(a, b, (((1,), (1,)), ((), ())), preferred_element_type=F32)


def _dot_tn(a, b):
    return lax.dot_general(a, b, (((0,), (0,)), ((), ())), preferred_element_type=F32)


def _split3(x):
    hi = x.astype(BF16)
    r = x - hi.astype(F32)
    mid = r.astype(BF16)
    lo = (r - mid.astype(F32)).astype(BF16)
    return hi, mid, lo


def _exact_left(m01, x):
    hi, mid, lo = _split3(x)
    return _dot(m01, lo) + _dot(m01, mid) + _dot(m01, hi)


def _exact_right(x, m01):
    hi, mid, lo = _split3(x)
    return _dot(lo, m01) + _dot(mid, m01) + _dot(hi, m01)


def _tri(n, lower):
    r = lax.broadcasted_iota(jnp.int32, (n, n), 0)
    c = lax.broadcasted_iota(jnp.int32, (n, n), 1)
    return jnp.where((r >= c) if lower else (r <= c), 1.0, 0.0).astype(BF16)


def _prev_rows_spec(tm, steps_per_seq):
    del steps_per_seq
    return pl.BlockSpec((HALO, D_MODEL), lambda i: (jnp.maximum(i * (tm // HALO) - 1, 0), 0))


def _normed_with_halo(x_ref, xp_ref, nw_ref, hs_ref, steps_per_seq):
    first = (pl.program_id(0) % steps_per_seq) == 0
    nw = nw_ref[...]
    hp = _rms(xp_ref[...]) * nw
    hs_ref[0:HALO, :] = jnp.where(first, 0.0, hp).astype(hs_ref.dtype)
    hs_ref[HALO:, :] = (_rms(x_ref[...]) * nw).astype(hs_ref.dtype)


def _ffn_kernel(x_ref, xp_ref, nw_ref, wu_ref, cw_ref, wd_ref, fw_ref, o_ref, hs_ref, a_ref,
                *, steps_per_seq, final_norm):
    _normed_with_halo(x_ref, xp_ref, nw_ref, hs_ref, steps_per_seq)

    def up(j):
        hs = hs_ref[...]
        lo = j * FFN_TILE
        return (_dot(hs, wu_ref[:, lo:lo + FFN_TILE]),
                _dot(hs, wu_ref[:, FFN_DIM + lo:FFN_DIM + lo + FFN_TILE]))

    def conv(u, lo):
        c = cw_ref[:, lo:lo + FFN_TILE]
        out = u[HALO:] * c[2:3] + c[3:4]
        out = out + pltpu.roll(u, 1, 0)[HALO:] * c[1:2]
        out = out + pltpu.roll(u, 2, 0)[HALO:] * c[0:1]
        return out

    u = up(0)
    for j in range(FFN_NT):
        nxt = up(j + 1) if j + 1 < FFN_NT else None
        g = conv(u[0], j * FFN_TILE)
        v = conv(u[1], FFN_DIM + j * FFN_TILE)
        a_ref[:, j * FFN_TILE:(j + 1) * FFN_TILE] = (g * _sigmoid(g) * v).astype(BF16)
        u = nxt
    y = x_ref[...] + _dot(a_ref[...], wd_ref[...])
    if final_norm:
        y = _rms(y) * fw_ref[...]
    o_ref[...] = y


def _ffn(x, nw, w_up, conv_w, conv_b, w_down, final_w, seq_len, final_norm, tm=512):
    t = x.shape[0]
    steps_per_seq = seq_len // tm
    cw = jnp.concatenate([conv_w, conv_b[None, :], jnp.zeros((4, 2 * FFN_DIM), F32)], axis=0)
    row = pl.BlockSpec((tm, D_MODEL), lambda i: (i, 0))
    return pl.pallas_call(
        functools.partial(_ffn_kernel, steps_per_seq=steps_per_seq, final_norm=final_norm),
        grid=(t // tm,),
        in_specs=[row, _prev_rows_spec(tm, steps_per_seq)] + [_vmem()] * 5,
        out_specs=row,
        out_shape=jax.ShapeDtypeStruct((t, D_MODEL), F32),
        scratch_shapes=[pltpu.VMEM((tm + HALO, D_MODEL), BF16), pltpu.VMEM((tm, FFN_DIM), BF16)],
        compiler_params=_params("parallel"),
        name="conv_ffn",
    )(x, x, nw.reshape(1, D_MODEL), w_up.astype(BF16), cw, w_down.astype(BF16), final_w.reshape(1, D_MODEL))


def _gla_kernel(x_ref, nw_ref, wq_ref, wk_ref, wv_ref, wr_ref, wg1_ref, wg2_ref, bg_ref, gnw_ref, wo_ref,
                o_ref, st_ref, q_s, k_s, v_s, la_s, o_s):
    tm = x_ref.shape[0]

    @pl.when(pl.program_id(1) == 0)
    def _():
        st_ref[...] = jnp.zeros_like(st_ref)

    x = x_ref[...]
    h = (_rms(x) * nw_ref[...]).astype(BF16)
    q_s[...] = _dot(h, wq_ref[...]).astype(BF16)
    k_s[...] = _dot(h, wk_ref[...])
    v_s[...] = _dot(h, wv_ref[...]).astype(BF16)
    glr = _dot(h, wg1_ref[...]).astype(BF16)
    gate = _dot(glr, wg2_ref[...]) + bg_ref[...]
    la_s[...] = -_softplus(-gate) * (1.0 / GLA_TAU)

    tril = _tri(GLA_CHUNK, lower=True)
    scale = GLA_DK ** -0.5
    nchunks = tm // GLA_CHUNK
    k_cols = [slice(hd * GLA_DK, (hd + 1) * GLA_DK) for hd in range(GLA_HEADS)]
    v_cols = [slice(hd * GLA_DV, (hd + 1) * GLA_DV) for hd in range(GLA_HEADS)]

    def chunk_update(c):
        rows = slice(c * GLA_CHUNK, (c + 1) * GLA_CHUNK)
        g = _exact_left(tril, la_s[rows, :])
        g_end = g[GLA_CHUNK - 1:GLA_CHUNK, :]
        kd = (k_s[rows, :] * jnp.exp(g_end - g)).astype(BF16)
        vc = v_s[rows, :]
        return [_dot_tn(vc[:, v_cols[hd]], kd[:, k_cols[hd]]) for hd in range(GLA_HEADS)], jnp.exp(g_end)

    state = [st_ref[hd] for hd in range(GLA_HEADS)]
    pending = chunk_update(0)
    for c in range(nchunks):
        upd, eg = pending
        if c + 1 < nchunks:
            pending = chunk_update(c + 1)
        rows = slice(c * GLA_CHUNK, (c + 1) * GLA_CHUNK)
        qc = q_s[rows, :]
        for hd in range(GLA_HEADS):
            state[hd] = state[hd] * eg[:, k_cols[hd]] + upd[hd]
            o_s[rows, v_cols[hd]] = _dot_nt(qc[:, k_cols[hd]], state[hd].astype(BF16)) * scale
    for hd in range(GLA_HEADS):
        st_ref[hd] = state[hd]

    r = _dot(h, wr_ref[...])
    gated = r * _sigmoid(r)
    gnw = gnw_ref[...]
    parts = []
    for hd in range(GLA_HEADS):
        vs = slice(hd * GLA_DV, (hd + 1) * GLA_DV)
        parts.append((_rms(o_s[:, vs]) * gnw[:, vs] * gated[:, vs]).astype(BF16))
    o_ref[...] = x + _dot(jnp.concatenate(parts, axis=-1), wo_ref[...])


def _gla(x, nw, w_in, w_gate2, b_gate, norm_w, w_out, batch, seq_len, tm=512):
    t = x.shape[0]
    spb = seq_len // tm
    wb = w_in.astype(BF16)
    wq, wk = wb[:, :GLA_HK], wb[:, GLA_HK:2 * GLA_HK]
    wv, wr = wb[:, 2 * GLA_HK:2 * GLA_HK + GLA_HV], wb[:, 2 * GLA_HK + GLA_HV:2 * GLA_HK + 2 * GLA_HV]
    wg1 = jnp.pad(wb[:, 2 * GLA_HK + 2 * GLA_HV:], ((0, 0), (0, 128 - GLA_RANK)))
    wg2 = jnp.pad(w_gate2.astype(BF16), ((0, 128 - GLA_RANK), (0, 0)))
    row = pl.BlockSpec((tm, D_MODEL), lambda b, i: (b * spb + i, 0))
    return pl.pallas_call(
        _gla_kernel,
        grid=(batch, spb),
        in_specs=[row] + [_vmem()] * 10,
        out_specs=row,
        out_shape=jax.ShapeDtypeStruct((t, D_MODEL), F32),
        scratch_shapes=[pltpu.VMEM((GLA_HEADS, GLA_DV, GLA_DK), F32),
                        pltpu.VMEM((tm, GLA_HK), BF16), pltpu.VMEM((tm, GLA_HK), F32),
                        pltpu.VMEM((tm, GLA_HV), BF16), pltpu.VMEM((tm, GLA_HK), F32),
                        pltpu.VMEM((tm, GLA_HV), F32)],
        compiler_params=_params("parallel", "arbitrary"),
        name="gla_mixer",
    )(x, nw.reshape(1, D_MODEL), wq, wk, wv, wr, wg1, wg2, b_gate.reshape(1, GLA_HK),
      norm_w.reshape(1, GLA_HV), w_out.astype(BF16))


def _pool_kernel(x_ref, xp_ref, nw_ref, w_ref, b_ref, sc_ref, o_ref, hs_ref, *, steps_per_seq):
    tm = x_ref.shape[0]
    _normed_with_halo(x_ref, xp_ref, nw_ref, hs_ref, steps_per_seq)
    pos = (pl.program_id(0) % steps_per_seq) * tm + lax.broadcasted_iota(jnp.int32, (tm, 1), 0)
    outs = []
    for gi, win in enumerate(POOL_WINDOWS):
        cols = slice(gi * POOL_GW, (gi + 1) * POOL_GW)
        cur = hs_ref[HALO:, cols]
        total = cur
        for back in range(1, win):
            total = total + hs_ref[HALO - back:HALO - back + tm, cols]
        count = jnp.minimum(pos + 1, win).astype(F32)
        dlt = (total / count - cur).astype(BF16)
        outs.append(_dot(dlt, w_ref[gi]) + b_ref[gi])
    o_ref[...] = x_ref[...] + jnp.concatenate(outs, axis=-1) * sc_ref[...]


def _pool(x, nw, w_grp, b_grp, scale, seq_len, tm=512):
    t = x.shape[0]
    steps_per_seq = seq_len // tm
    row = pl.BlockSpec((tm, D_MODEL), lambda i: (i, 0))
    return pl.pallas_call(
        functools.partial(_pool_kernel, steps_per_seq=steps_per_seq),
        grid=(t // tm,),
        in_specs=[row, _prev_rows_spec(tm, steps_per_seq), _vmem(), _vmem(), _vmem(), _vmem()],
        out_specs=row,
        out_shape=jax.ShapeDtypeStruct((t, D_MODEL), F32),
        scratch_shapes=[pltpu.VMEM((tm + HALO, D_MODEL), F32)],
        compiler_params=_params("parallel"),
        name="pool_mixer",
    )(x, x, nw.reshape(1, D_MODEL), w_grp.astype(BF16), b_grp.reshape(len(POOL_WINDOWS), 1, POOL_GW),
      scale.reshape(1, D_MODEL))


def _ssd_proj_kernel(x_ref, xp_ref, nw_ref, wz_ref, wx_ref, wdt_ref, wdtt_ref, cw_ref, z_ref, xs_ref, bm_ref,
                     cm_ref, dt_ref, dtt_ref, hs_ref, u0_ref, u1_ref, *, steps_per_seq):
    tm = x_ref.shape[0]
    _normed_with_halo(x_ref, xp_ref, nw_ref, hs_ref, steps_per_seq)
    h = hs_ref[HALO:, :]
    tile, ztile = SSD_PROJ_TILE, 2 * SSD_PROJ_TILE
    ntiles = (SSD_DINNER + 2 * SSD_GN) // tile
    bufs = (u0_ref, u1_ref)

    def up(j):
        bufs[j % 2][...] = _dot(hs_ref[...], wx_ref[:, j * tile:(j + 1) * tile])

    def conv(j):
        lo = j * tile
        c = cw_ref[:, lo:lo + tile]
        if lo < SSD_DINNER:
            dst, off = xs_ref, lo
        elif lo < SSD_DINNER + SSD_GN:
            dst, off = bm_ref, lo - SSD_DINNER
        else:
            dst, off = cm_ref, lo - SSD_DINNER - SSD_GN
        src = bufs[j % 2]
        acc = src[HALO:, :] * c[3:4] + c[4:5]
        for back in range(1, SSD_CONV):
            acc = acc + src[HALO - back:HALO - back + tm, :] * c[3 - back:4 - back]
        dst[:, off:off + tile] = (acc * _sigmoid(acc)).astype(BF16)

    up(0)
    for j in range(ntiles):
        if j + 1 < ntiles:
            up(j + 1)
        if j % 3 == 2:
            zc = slice((j // 3) * ztile, (j // 3 + 1) * ztile)
            z_ref[:, zc] = _dot(h, wz_ref[:, zc]).astype(z_ref.dtype)
        conv(j)
    dt_ref[...] = _dot(h, wdt_ref[...])
    dtt_ref[...] = _dot_nt(wdtt_ref[...], h)


def _ssd_core_kernel(x_ref, z_ref, xs_ref, bm_ref, cm_ref, dt_ref, dtt_ref, dtb_r, dtb_c, alog_r, alog_c,
                     dsk_ref, gnw_ref, rexp_ref, wo_ref, o_ref, st_ref, y_s, ex_s, we_s):
    tm, q = x_ref.shape[0], SSD_Q
    gw = SSD_HPG * SSD_P

    @pl.when(pl.program_id(1) == 0)
    def _():
        st_ref[...] = jnp.zeros_like(st_ref)

    dt = _softplus(dt_ref[...] + dtb_r[...])
    dtt = _softplus(dtt_ref[...] + dtb_c[...])
    a = dt * -jnp.exp(alog_r[...])
    at = dtt * -jnp.exp(alog_c[...])
    tri_l, tri_u = _tri(q, lower=True), _tri(q, lower=False)
    acs, acst, per_row = [], [], []
    for b in range(tm // q):
        rows = slice(b * q, (b + 1) * q)
        acs_b = _exact_left(tri_l, a[rows])
        acs.append(acs_b)
        acst.append(_exact_right(at[:, rows], tri_u))
        per_row.append(jnp.concatenate([jnp.exp(acs_b), dt[rows] * jnp.exp(acs_b[q - 1:q] - acs_b)], axis=1))
    ew = jnp.concatenate(per_row, axis=0)
    ew_hi = ew.astype(BF16)
    ew_parts = jnp.concatenate([ew_hi, (ew - ew_hi.astype(F32)).astype(BF16)], axis=1)
    ex_s[...] = _dot(ew_parts, rexp_ref[:, :SSD_DINNER])
    we_s[...] = (xs_ref[...].astype(F32) * _dot(ew_parts, rexp_ref[:, SSD_DINNER:])).astype(BF16)

    causal = (lax.broadcasted_iota(jnp.int32, (q, q), 0) >= lax.broadcasted_iota(jnp.int32, (q, q), 1))
    low_half = lax.broadcasted_iota(jnp.int32, (1, 2 * SSD_P), 1) < SSD_P
    state = [st_ref[g] for g in range(SSD_GROUPS)]
    for b in range(tm // q):
        rows = slice(b * q, (b + 1) * q)
        for g in range(SSD_GROUPS):
            ns = slice(g * SSD_N, (g + 1) * SSD_N)
            gs = slice(g * gw, (g + 1) * gw)
            bg = bm_ref[rows, ns]
            cg = cm_ref[rows, ns]
            cb = _dot_nt(cg, bg)
            update = _dot_tn(bg, we_s[rows, gs])
            pieces = []
            for pair in range(SSD_HPG // 2):
                cols = slice(g * gw + pair * 2 * SSD_P, g * gw + (pair + 1) * 2 * SSD_P)
                xp = xs_ref[rows, cols]
                ms = []
                for half in range(2):
                    hh = g * SSD_HPG + pair * 2 + half
                    seg = acs[b][:, hh:hh + 1] - acst[b][hh:hh + 1, :]
                    decay = jnp.exp(jnp.where(causal, seg, -1e30))
                    ms.append((cb * decay * dtt[hh:hh + 1, rows]).astype(BF16))
                xp2 = jnp.concatenate([jnp.where(low_half, xp, jnp.zeros_like(xp)),
                                       jnp.where(low_half, jnp.zeros_like(xp), xp)], axis=0)
                pieces.append(_dot(jnp.concatenate(ms, axis=1), xp2))
            y_s[rows, gs] = (_dot(cg, state[g].astype(BF16)) * ex_s[rows, gs]
                             + jnp.concatenate(pieces, axis=1))
            state[g] = state[g] * ex_s[(b + 1) * q - 1:(b + 1) * q, gs] + update
    for g in range(SSD_GROUPS):
        st_ref[g] = state[g]

    gnw = gnw_ref[...]
    dsk = dsk_ref[...]
    parts = []
    for g in range(SSD_GROUPS):
        gs = slice(g * gw, (g + 1) * gw)
        zz = z_ref[:, gs].astype(F32)
        y = (y_s[:, gs] + dsk[:, gs] * xs_ref[:, gs].astype(F32)) * (zz * _sigmoid(zz))
        parts.append((_rms(y) * gnw[:, gs]).astype(BF16))
    o_ref[...] = x_ref[...] + _dot(jnp.concatenate(parts, axis=-1), wo_ref[...])


def _ssd(x, nw, w_in, conv_w, conv_b, dt_bias, a_log, d_skip, norm_w, w_out, batch, seq_len, tm=512):
    t = x.shape[0]
    steps_per_seq = seq_len // tm
    conv_dim = SSD_DINNER + 2 * SSD_GN
    wb = w_in.astype(BF16)
    wz, wx, wdt = wb[:, :SSD_DINNER], wb[:, SSD_DINNER:SSD_DINNER + conv_dim], wb[:, SSD_DINNER + conv_dim:]
    cw = jnp.concatenate([conv_w, conv_b[None, :], jnp.zeros((3, conv_dim), F32)], axis=0)
    row = pl.BlockSpec((tm, D_MODEL), lambda i: (i, 0))

    def out_row(n):
        return pl.BlockSpec((tm, n), lambda i: (i, 0))

    z, xs, bm, cm, dt, dtt = pl.pallas_call(
        functools.partial(_ssd_proj_kernel, steps_per_seq=steps_per_seq),
        grid=(t // tm,),
        in_specs=[row, _prev_rows_spec(tm, steps_per_seq)] + [_vmem()] * 6,
        out_specs=[out_row(SSD_DINNER), out_row(SSD_DINNER), out_row(SSD_GN), out_row(SSD_GN),
                   out_row(SSD_HEADS), pl.BlockSpec((SSD_HEADS, tm), lambda i: (0, i))],
        out_shape=[jax.ShapeDtypeStruct((t, SSD_DINNER), BF16), jax.ShapeDtypeStruct((t, SSD_DINNER), BF16),
                   jax.ShapeDtypeStruct((t, SSD_GN), BF16), jax.ShapeDtypeStruct((t, SSD_GN), BF16),
                   jax.ShapeDtypeStruct((t, SSD_HEADS), F32), jax.ShapeDtypeStruct((SSD_HEADS, t), F32)],
        scratch_shapes=[pltpu.VMEM((tm + HALO, D_MODEL), BF16), pltpu.VMEM((tm + HALO, SSD_PROJ_TILE), F32),
                        pltpu.VMEM((tm + HALO, SSD_PROJ_TILE), F32)],
        compiler_params=_params("parallel"),
        name="ssd_proj",
    )(x, x, nw.reshape(1, D_MODEL), wz, wx, wdt, wdt.T, cw)

    spb = seq_len // tm
    spread = jnp.repeat(jnp.eye(SSD_HEADS, dtype=BF16), SSD_P, axis=1)
    zero = jnp.zeros_like(spread)
    top = jnp.concatenate([jnp.concatenate([spread, zero], axis=1), jnp.concatenate([zero, spread], axis=1)], axis=0)
    rexp = jnp.concatenate([top, top], axis=0)
    dsk = jnp.repeat(d_skip, SSD_P).reshape(1, SSD_DINNER)

    def blk(n):
        return pl.BlockSpec((tm, n), lambda b, i: (b * spb + i, 0))

    return pl.pallas_call(
        _ssd_core_kernel,
        grid=(batch, spb),
        in_specs=[blk(D_MODEL), blk(SSD_DINNER), blk(SSD_DINNER), blk(SSD_GN), blk(SSD_GN), blk(SSD_HEADS),
                  pl.BlockSpec((SSD_HEADS, tm), lambda b, i: (0, b * spb + i))] + [_vmem()] * 8,
        out_specs=blk(D_MODEL),
        out_shape=jax.ShapeDtypeStruct((t, D_MODEL), F32),
        scratch_shapes=[pltpu.VMEM((SSD_GROUPS, SSD_N, SSD_HPG * SSD_P), F32),
                        pltpu.VMEM((tm, SSD_DINNER), F32), pltpu.VMEM((tm, SSD_DINNER), F32),
                        pltpu.VMEM((tm, SSD_DINNER), BF16)],
        compiler_params=_params("parallel", "arbitrary"),
        name="ssd_core",
    )(x, z, xs, bm, cm, dt, dtt, dt_bias.reshape(1, SSD_HEADS), dt_bias.reshape(SSD_HEADS, 1),
      a_log.reshape(1, SSD_HEADS), a_log.reshape(SSD_HEADS, 1), dsk, norm_w.reshape(1, SSD_DINNER), rexp,
      w_out.astype(BF16))


def _sb_proj_kernel(x_ref, nw_ref, wq_ref, wk_ref, wv_ref, q_ref, k_ref, v_ref):
    h = (_rms(x_ref[...]) * nw_ref[...]).astype(BF16)
    q_ref[...] = (_dot(h, wq_ref[...]) * (SB_DH ** -0.5 * LOG2E)).astype(BF16)
    k_ref[...] = _dot(h, wk_ref[...]).astype(BF16)
    v_ref[...] = _dot(h, wv_ref[...]).astype(BF16)


def _sb_attn_kernel(q_ref, k_ref, v_ref, o_ref, acc_ref, z_s, a_s):
    t, nh, ns = SB_T, SB_HPS, SB_SLABS
    w = nh * SB_DH
    nc = ns * nh
    i = pl.program_id(2)
    head_of_lane = lax.broadcasted_iota(jnp.int32, (1, w), 1) // SB_DH

    def per_head(a):
        return jnp.concatenate([jnp.where(head_of_lane == h, a, jnp.zeros_like(a)) for h in range(nh)], axis=0)

    def key_rows(j):
        return pl.ds(pl.multiple_of(j * t, t), t)

    qs = [per_head(q_ref[0, :, sl * w:(sl + 1) * w]) for sl in range(ns)]
    upper = _tri(t, lower=True)
    strict = (lax.broadcasted_iota(jnp.int32, (t, t), 0) > lax.broadcasted_iota(jnp.int32, (t, t), 1))

    def logits(j, sl):
        return _dot_nt(qs[sl], k_ref[0, key_rows(j), sl * w:(sl + 1) * w])

    def survival(z_of, carry, masked, j_next):
        rcs, ys = [], []
        rows_next = key_rows(jnp.maximum(j_next, 0))
        k_next = [k_ref[0, rows_next, sl * w:(sl + 1) * w] for sl in range(ns)]
        for c in range(nc):
            z = z_of(c)
            sp = _softplus_base2(z)
            if masked:
                sp = jnp.where(strict, sp, 0.0)
            rcs.append(_dot(sp.astype(BF16), upper))
            ys.append(z - carry[c])
            sl, h = divmod(c, nh)
            z_s[c * t:(c + 1) * t, :] = _dot_nt(qs[sl][h * t:(h + 1) * t], k_next[sl])
        return rcs, ys

    def weights(rcs, ys, masked, chains):
        for c in chains:
            att = jnp.exp2(ys[c] - rcs[c])
            if masked:
                att = jnp.where(strict, att, 0.0)
            a_s[:, c * t:(c + 1) * t] = att.astype(BF16)

    def weighted_values(j):
        return [_dot(a_s[:, sl * nh * t:(sl + 1) * nh * t], per_head(v_ref[0, key_rows(j), sl * w:(sl + 1) * w]))
                for sl in range(ns)]

    zero = jnp.zeros((t, 1), F32)
    zz = [logits(i, sl) for sl in range(ns)]
    rcs, ys = survival(lambda c: zz[c // nh][(c % nh) * t:(c % nh + 1) * t], (zero,) * nc, True, i - 1)
    weights(rcs, ys, True, range(nc))
    acc_ref[...] = jnp.zeros_like(acc_ref)

    def step(s, carry):
        j = i - 1 - s
        pv = weighted_values(j + 1)
        rcs, ys = survival(lambda c: z_s[c * t:(c + 1) * t, :], carry, False, j - 1)
        for sl in range(ns):
            acc_ref[:, sl * w:(sl + 1) * w] += pv[sl]
        weights(rcs, ys, False, range(nc))
        return tuple(carry[c] + rcs[c][:, 0:1] for c in range(nc))

    lax.fori_loop(0, i, step, tuple(rcs[c][:, 0:1] for c in range(nc)))
    pv = weighted_values(0)
    for sl in range(ns):
        o_ref[0, :, sl * w:(sl + 1) * w] = (acc_ref[:, sl * w:(sl + 1) * w] + pv[sl]).astype(o_ref.dtype)


def _out_proj_kernel(x_ref, a_ref, w_ref, o_ref):
    o_ref[...] = x_ref[...] + _dot(a_ref[...], w_ref[...])


def _sb(x, nw, w_qkv, w_out, batch, seq_len, tm=512):
    t = x.shape[0]
    wb = w_qkv.astype(BF16)
    row = pl.BlockSpec((tm, D_MODEL), lambda i: (i, 0))
    qkv_shape = jax.ShapeDtypeStruct((t, D_MODEL), BF16)
    q, k, v = pl.pallas_call(
        _sb_proj_kernel,
        grid=(t // tm,),
        in_specs=[row] + [_vmem()] * 4,
        out_specs=[row, row, row],
        out_shape=[qkv_shape, qkv_shape, qkv_shape],
        compiler_params=_params("parallel"),
        name="sb_proj",
    )(x, nw.reshape(1, D_MODEL), wb[:, :D_MODEL], wb[:, D_MODEL:2 * D_MODEL], wb[:, 2 * D_MODEL:])

    shape3 = (batch, seq_len, D_MODEL)
    lanes = SB_SLABS * SB_HPS * SB_DH
    chains = SB_SLABS * SB_HPS
    qblk = pl.BlockSpec((1, SB_T, lanes), lambda b, p, i: (b, i, p))
    kvblk = pl.BlockSpec((1, seq_len, lanes), lambda b, p, i: (b, 0, p))
    att = pl.pallas_call(
        _sb_attn_kernel,
        grid=(batch, SB_HEADS // chains, seq_len // SB_T),
        in_specs=[qblk, kvblk, kvblk],
        out_specs=qblk,
        out_shape=jax.ShapeDtypeStruct(shape3, BF16),
        scratch_shapes=[pltpu.VMEM((SB_T, lanes), F32), pltpu.VMEM((chains * SB_T, SB_T), F32),
                        pltpu.VMEM((SB_T, chains * SB_T), BF16)],
        compiler_params=_params("parallel", "parallel", "arbitrary"),
        name="sb_attention",
    )(q.reshape(shape3), k.reshape(shape3), v.reshape(shape3))

    return pl.pallas_call(
        _out_proj_kernel,
        grid=(t // tm,),
        in_specs=[row, row, _vmem()],
        out_specs=row,
        out_shape=jax.ShapeDtypeStruct((t, D_MODEL), F32),
        compiler_params=_params("parallel"),
        name="sb_out_proj",
    )(x, att.reshape(t, D_MODEL), w_out.astype(BF16))


def kernel(x, mix_norm_w, ffn_norm_w, final_norm_w, gla_w_in, gla_w_gate2, gla_b_gate, gla_norm_w, gla_w_out, pool_w, pool_b, pool_scale, ssd_w_in, ssd_conv_w, ssd_conv_b, ssd_dt_bias, ssd_a_log, ssd_d, ssd_norm_w, ssd_w_out, sb_w_qkv, sb_w_out, ffn_w_up, ffn_conv_w, ffn_conv_b, ffn_w_down):
    batch, seq_len, d = x.shape
    assert d == D_MODEL
    depth = mix_norm_w.shape[0]
    xf = x.reshape(batch * seq_len, d)
    for i in range(depth):
        m, j = i % 4, i // 4
        if m == 0:
            xf = _gla(xf, mix_norm_w[i], gla_w_in[j], gla_w_gate2[j], gla_b_gate[j], gla_norm_w[j], gla_w_out[j],
                      batch, seq_len)
        elif m == 1:
            xf = _pool(xf, mix_norm_w[i], pool_w[j], pool_b[j], pool_scale[j], seq_len)
        elif m == 2:
            xf = _ssd(xf, mix_norm_w[i], ssd_w_in[j], ssd_conv_w[j], ssd_conv_b[j], ssd_dt_bias[j], ssd_a_log[j],
                      ssd_d[j], ssd_norm_w[j], ssd_w_out[j], batch, seq_len)
        else:
            xf = _sb(xf, mix_norm_w[i], sb_w_qkv[j], sb_w_out[j], batch, seq_len)
        xf = _ffn(xf, ffn_norm_w[i], ffn_w_up[i], ffn_conv_w[i], ffn_conv_b[i], ffn_w_down[i], final_norm_w,
                  seq_len, final_norm=(i == depth - 1))
    return xf.reshape(batch, seq_len, d)
```

```python
import functools

import jax
import jax.numpy as jnp
from jax import lax
from jax.experimental import pallas as pl
from jax.experimental.pallas import tpu as pltpu

F32 = jnp.float32
BF16 = jnp.bfloat16

EPS = 1e-6
LOG2E = 1.4426950408889634
D_MODEL = 1024
HALO = 16
VMEM_LIMIT = 56 * 1024 * 1024

GLA_HEADS, GLA_DK, GLA_DV, GLA_RANK, GLA_TAU, GLA_CHUNK = 4, 128, 256, 16, 16.0, 64
GLA_HK, GLA_HV = GLA_HEADS * GLA_DK, GLA_HEADS * GLA_DV
POOL_WINDOWS, POOL_GW = (2, 4, 8, 16), 256
SSD_DINNER, SSD_P, SSD_HEADS, SSD_GROUPS, SSD_HPG, SSD_N, SSD_CONV = 2048, 64, 32, 4, 8, 128, 4
SSD_GN = SSD_GROUPS * SSD_N
SSD_Q = 128
SSD_PROJ_TILE = 256
SB_HEADS, SB_DH, SB_T = 16, 64, 256
SB_HPS = 4
SB_SLABS = 1
FFN_DIM, FFN_TILE, FFN_CONV = 2816, 256, 3
FFN_NT = FFN_DIM // FFN_TILE


def _params(*sem):
    return pltpu.CompilerParams(dimension_semantics=sem, vmem_limit_bytes=VMEM_LIMIT)


def _vmem():
    return pl.BlockSpec(memory_space=pltpu.VMEM)


def _rms(x):
    return x * lax.rsqrt(jnp.mean(x * x, axis=-1, keepdims=True) + EPS)


def _softplus(x):
    return jnp.maximum(x, 0.0) + jnp.log(1.0 + jnp.exp(-jnp.abs(x)))


def _softplus_base2(x):
    return jnp.where(x > 64.0, x, jnp.log(1.0 + jnp.exp2(x)) * LOG2E)


def _sigmoid(x):
    return 1.0 / (1.0 + jnp.exp(-x))


def _dot(a, b):
    return jnp.dot(a, b, preferred_element_type=F32)


def _dot_nt(a, b):
    return lax.dot_general(a, b, (((1,), (1,)), ((), ())), preferred_element_type=F32)


def _dot_tn(a, b):
    return lax.dot_general(a, b, (((0,), (0,)), ((), ())), preferred_element_type=F32)


def _split3(x):
    hi = x.astype(BF16)
    r = x - hi.astype(F32)
    mid = r.astype(BF16)
    lo = (r - mid.astype(F32)).astype(BF16)
    return hi, mid, lo


def _exact_left(m01, x):
    hi, mid, lo = _split3(x)
    return _dot(m01, lo) + _dot(m01, mid) + _dot(m01, hi)


def _exact_right(x, m01):
    hi, mid, lo = _split3(x)
    return _dot(lo, m01) + _dot(mid, m01) + _dot(hi, m01)


def _tri(n, lower):
    r = lax.broadcasted_iota(jnp.int32, (n, n), 0)
    c = lax.broadcasted_iota(jnp.int32, (n, n), 1)
    return jnp.where((r >= c) if lower else (r <= c), 1.0, 0.0).astype(BF16)


def _prev_rows_spec(tm, steps_per_seq):
    del steps_per_seq
    return pl.BlockSpec((HALO, D_MODEL), lambda i: (jnp.maximum(i * (tm // HALO) - 1, 0), 0))


def _normed_with_halo(x_ref, xp_ref, nw_ref, hs_ref, steps_per_seq):
    first = (pl.program_id(0) % steps_per_seq) == 0
    nw = nw_ref[...]
    hp = _rms(xp_ref[...]) * nw
    hs_ref[0:HALO, :] = jnp.where(first, 0.0, hp).astype(hs_ref.dtype)
    hs_ref[HALO:, :] = (_rms(x_ref[...]) * nw).astype(hs_ref.dtype)


def _ffn_kernel(x_ref, xp_ref, nw_ref, wu_ref, cw_ref, wd_ref, fw_ref, o_ref, hs_ref, a_ref,
                *, steps_per_seq, final_norm):
    _normed_with_halo(x_ref, xp_ref, nw_ref, hs_ref, steps_per_seq)

    def up(j):
        hs = hs_ref[...]
        lo = j * FFN_TILE
        return (_dot(hs, wu_ref[:, lo:lo + FFN_TILE]),
                _dot(hs, wu_ref[:, FFN_DIM + lo:FFN_DIM + lo + FFN_TILE]))

    def conv(u, lo):
        c = cw_ref[:, lo:lo + FFN_TILE]
        out = u[HALO:] * c[2:3] + c[3:4]
        out = out + pltpu.roll(u, 1, 0)[HALO:] * c[1:2]
        out = out + pltpu.roll(u, 2, 0)[HALO:] * c[0:1]
        return out

    u = up(0)
    for j in range(FFN_NT):
        nxt = up(j + 1) if j + 1 < FFN_NT else None
        g = conv(u[0], j * FFN_TILE)
        v = conv(u[1], FFN_DIM + j * FFN_TILE)
        a_ref[:, j * FFN_TILE:(j + 1) * FFN_TILE] = (g * _sigmoid(g) * v).astype(BF16)
        u = nxt
    y = x_ref[...] + _dot(a_ref[...], wd_ref[...])
    if final_norm:
        y = _rms(y) * fw_ref[...]
    o_ref[...] = y


def _ffn(x, nw, w_up, conv_w, conv_b, w_down, final_w, seq_len, final_norm, tm=512):
    t = x.shape[0]
    steps_per_seq = seq_len // tm
    cw = jnp.concatenate([conv_w, conv_b[None, :], jnp.zeros((4, 2 * FFN_DIM), F32)], axis=0)
    row = pl.BlockSpec((tm, D_MODEL), lambda i: (i, 0))
    return pl.pallas_call(
        functools.partial(_ffn_kernel, steps_per_seq=steps_per_seq, final_norm=final_norm),
        grid=(t // tm,),
        in_specs=[row, _prev_rows_spec(tm, steps_per_seq)] + [_vmem()] * 5,
        out_specs=row,
        out_shape=jax.ShapeDtypeStruct((t, D_MODEL), F32),
        scratch_shapes=[pltpu.VMEM((tm + HALO, D_MODEL), BF16), pltpu.VMEM((tm, FFN_DIM), BF16)],
        compiler_params=_params("parallel"),
        name="conv_ffn",
    )(x, x, nw.reshape(1, D_MODEL), w_up.astype(BF16), cw, w_down.astype(BF16), final_w.reshape(1, D_MODEL))


def _gla_kernel(x_ref, nw_ref, wq_ref, wk_ref, wv_ref, wr_ref, wg1_ref, wg2_ref, bg_ref, gnw_ref, wo_ref,
                o_ref, st_ref, q_s, k_s, v_s, la_s, o_s):
    tm = x_ref.shape[0]

    @pl.when(pl.program_id(1) == 0)
    def _():
        st_ref[...] = jnp.zeros_like(st_ref)

    x = x_ref[...]
    h = (_rms(x) * nw_ref[...]).astype(BF16)
    q_s[...] = _dot(h, wq_ref[...]).astype(BF16)
    k_s[...] = _dot(h, wk_ref[...])
    v_s[...] = _dot(h, wv_ref[...]).astype(BF16)
    glr = _dot(h, wg1_ref[...]).astype(BF16)
    gate = _dot(glr, wg2_ref[...]) + bg_ref[...]
    la_s[...] = -_softplus(-gate) * (1.0 / GLA_TAU)

    tril = _tri(GLA_CHUNK, lower=True)
    scale = GLA_DK ** -0.5
    nchunks = tm // GLA_CHUNK
    k_cols = [slice(hd * GLA_DK, (hd + 1) * GLA_DK) for hd in range(GLA_HEADS)]
    v_cols = [slice(hd * GLA_DV, (hd + 1) * GLA_DV) for hd in range(GLA_HEADS)]

    def chunk_update(c):
        rows = slice(c * GLA_CHUNK, (c + 1) * GLA_CHUNK)
        g = _exact_left(tril, la_s[rows, :])
        g_end = g[GLA_CHUNK - 1:GLA_CHUNK, :]
        kd = (k_s[rows, :] * jnp.exp(g_end - g)).astype(BF16)
        vc = v_s[rows, :]
        return [_dot_tn(vc[:, v_cols[hd]], kd[:, k_cols[hd]]) for hd in range(GLA_HEADS)], jnp.exp(g_end)

    state = [st_ref[hd] for hd in range(GLA_HEADS)]
    pending = chunk_update(0)
    for c in range(nchunks):
        upd, eg = pending
        if c + 1 < nchunks:
            pending = chunk_update(c + 1)
        rows = slice(c * GLA_CHUNK, (c + 1) * GLA_CHUNK)
        qc = q_s[rows, :]
        for hd in range(GLA_HEADS):
            state[hd] = state[hd] * eg[:, k_cols[hd]] + upd[hd]
            o_s[rows, v_cols[hd]] = _dot_nt(qc[:, k_cols[hd]], state[hd].astype(BF16)) * scale
    for hd in range(GLA_HEADS):
        st_ref[hd] = state[hd]

    r = _dot(h, wr_ref[...])
    gated = r * _sigmoid(r)
    gnw = gnw_ref[...]
    parts = []
    for hd in range(GLA_HEADS):
        vs = slice(hd * GLA_DV, (hd + 1) * GLA_DV)
        parts.append((_rms(o_s[:, vs]) * gnw[:, vs] * gated[:, vs]).astype(BF16))
    o_ref[...] = x + _dot(jnp.concatenate(parts, axis=-1), wo_ref[...])


def _gla(x, nw, w_in, w_gate2, b_gate, norm_w, w_out, batch, seq_len, tm=512):
    t = x.shape[0]
    spb = seq_len // tm
    wb = w_in.astype(BF16)
    wq, wk = wb[:, :GLA_HK], wb[:, GLA_HK:2 * GLA_HK]
    wv, wr = wb[:, 2 * GLA_HK:2 * GLA_HK + GLA_HV], wb[:, 2 * GLA_HK + GLA_HV:2 * GLA_HK + 2 * GLA_HV]
    wg1 = jnp.pad(wb[:, 2 * GLA_HK + 2 * GLA_HV:], ((0, 0), (0, 128 - GLA_RANK)))
    wg2 = jnp.pad(w_gate2.astype(BF16), ((0, 128 - GLA_RANK), (0, 0)))
    row = pl.BlockSpec((tm, D_MODEL), lambda b, i: (b * spb + i, 0))
    return pl.pallas_call(
        _gla_kernel,
        grid=(batch, spb),
        in_specs=[row] + [_vmem()] * 10,
        out_specs=row,
        out_shape=jax.ShapeDtypeStruct((t, D_MODEL), F32),
        scratch_shapes=[pltpu.VMEM((GLA_HEADS, GLA_DV, GLA_DK), F32),
                        pltpu.VMEM((tm, GLA_HK), BF16), pltpu.VMEM((tm, GLA_HK), F32),
                        pltpu.VMEM((tm, GLA_HV), BF16), pltpu.VMEM((tm, GLA_HK), F32),
                        pltpu.VMEM((tm, GLA_HV), F32)],
        compiler_params=_params("parallel", "arbitrary"),
        name="gla_mixer",
    )(x, nw.reshape(1, D_MODEL), wq, wk, wv, wr, wg1, wg2, b_gate.reshape(1, GLA_HK),
      norm_w.reshape(1, GLA_HV), w_out.astype(BF16))


def _pool_kernel(x_ref, xp_ref, nw_ref, w_ref, b_ref, sc_ref, o_ref, hs_ref, *, steps_per_seq):
    tm = x_ref.shape[0]
    _normed_with_halo(x_ref, xp_ref, nw_ref, hs_ref, steps_per_seq)
    pos = (pl.program_id(0) % steps_per_seq) * tm + lax.broadcasted_iota(jnp.int32, (tm, 1), 0)
    outs = []
    for gi, win in enumerate(POOL_WINDOWS):
        cols = slice(gi * POOL_GW, (gi + 1) * POOL_GW)
        cur = hs_ref[HALO:, cols]
        total = cur
        for back in range(1, win):
            total = total + hs_ref[HALO - back:HALO - back + tm, cols]
        count = jnp.minimum(pos + 1, win).astype(F32)
        dlt = (total / count - cur).astype(BF16)
        outs.append(_dot(dlt, w_ref[gi]) + b_ref[gi])
    o_ref[...] = x_ref[...] + jnp.concatenate(outs, axis=-1) * sc_ref[...]


def _pool(x, nw, w_grp, b_grp, scale, seq_len, tm=512):
    t = x.shape[0]
    steps_per_seq = seq_len // tm
    row = pl.BlockSpec((tm, D_MODEL), lambda i: (i, 0))
    return pl.pallas_call(
        functools.partial(_pool_kernel, steps_per_seq=steps_per_seq),
        grid=(t // tm,),
        in_specs=[row, _prev_rows_spec(tm, steps_per_seq), _vmem(), _vmem(), _vmem(), _vmem()],
        out_specs=row,
        out_shape=jax.ShapeDtypeStruct((t, D_MODEL), F32),
        scratch_shapes=[pltpu.VMEM((tm + HALO, D_MODEL), F32)],
        compiler_params=_params("parallel"),
        name="pool_mixer",
    )(x, x, nw.reshape(1, D_MODEL), w_grp.astype(BF16), b_grp.reshape(len(POOL_WINDOWS), 1, POOL_GW),
      scale.reshape(1, D_MODEL))


def _ssd_proj_kernel(x_ref, xp_ref, nw_ref, wz_ref, wx_ref, wdt_ref, wdtt_ref, cw_ref, z_ref, xs_ref, bm_ref,
                     cm_ref, dt_ref, dtt_ref, hs_ref, u0_ref, u1_ref, *, steps_per_seq):
    tm = x_ref.shape[0]
    _normed_with_halo(x_ref, xp_ref, nw_ref, hs_ref, steps_per_seq)
    h = hs_ref[HALO:, :]
    tile, ztile = SSD_PROJ_TILE, 2 * SSD_PROJ_TILE
    ntiles = (SSD_DINNER + 2 * SSD_GN) // tile
    bufs = (u0_ref, u1_ref)

    def up(j):
        bufs[j % 2][...] = _dot(hs_ref[...], wx_ref[:, j * tile:(j + 1) * tile])

    def conv(j):
        lo = j * tile
        c = cw_ref[:, lo:lo + tile]
        if lo < SSD_DINNER:
            dst, off = xs_ref, lo
        elif lo < SSD_DINNER + SSD_GN:
            dst, off = bm_ref, lo - SSD_DINNER
        else:
            dst, off = cm_ref, lo - SSD_DINNER - SSD_GN
        src = bufs[j % 2]
        acc = src[HALO:, :] * c[3:4] + c[4:5]
        for back in range(1, SSD_CONV):
            acc = acc + src[HALO - back:HALO - back + tm, :] * c[3 - back:4 - back]
        dst[:, off:off + tile] = (acc * _sigmoid(acc)).astype(BF16)

    up(0)
    for j in range(ntiles):
        if j + 1 < ntiles:
            up(j + 1)
        if j % 3 == 2:
            zc = slice((j // 3) * ztile, (j // 3 + 1) * ztile)
            z_ref[:, zc] = _dot(h, wz_ref[:, zc]).astype(z_ref.dtype)
        conv(j)
    dt_ref[...] = _dot(h, wdt_ref[...])
    dtt_ref[...] = _dot_nt(wdtt_ref[...], h)


def _ssd_core_kernel(x_ref, z_ref, xs_ref, bm_ref, cm_ref, dt_ref, dtt_ref, dtb_r, dtb_c, alog_r, alog_c,
                     dsk_ref, gnw_ref, rexp_ref, wo_ref, o_ref, st_ref, y_s, ex_s, we_s):
    tm, q = x_ref.shape[0], SSD_Q
    gw = SSD_HPG * SSD_P

    @pl.when(pl.program_id(1) == 0)
    def _():
        st_ref[...] = jnp.zeros_like(st_ref)

    dt = _softplus(dt_ref[...] + dtb_r[...])
    dtt = _softplus(dtt_ref[...] + dtb_c[...])
    a = dt * -jnp.exp(alog_r[...])
    at = dtt * -jnp.exp(alog_c[...])
    tri_l, tri_u = _tri(q, lower=True), _tri(q, lower=False)
    acs, acst, per_row = [], [], []
    for b in range(tm // q):
        rows = slice(b * q, (b + 1) * q)
        acs_b = _exact_left(tri_l, a[rows])
        acs.append(acs_b)
        acst.append(_exact_right(at[:, rows], tri_u))
        per_row.append(jnp.concatenate([jnp.exp(acs_b), dt[rows] * jnp.exp(acs_b[q - 1:q] - acs_b)], axis=1))
    ew = jnp.concatenate(per_row, axis=0)
    ew_hi = ew.astype(BF16)
    ew_parts = jnp.concatenate([ew_hi, (ew - ew_hi.astype(F32)).astype(BF16)], axis=1)
    ex_s[...] = _dot(ew_parts, rexp_ref[:, :SSD_DINNER])
    we_s[...] = (xs_ref[...].astype(F32) * _dot(ew_parts, rexp_ref[:, SSD_DINNER:])).astype(BF16)

    causal = (lax.broadcasted_iota(jnp.int32, (q, q), 0) >= lax.broadcasted_iota(jnp.int32, (q, q), 1))
    low_half = lax.broadcasted_iota(jnp.int32, (1, 2 * SSD_P), 1) < SSD_P
    state = [st_ref[g] for g in range(SSD_GROUPS)]
    for b in range(tm // q):
        rows = slice(b * q, (b + 1) * q)
        for g in range(SSD_GROUPS):
            ns = slice(g * SSD_N, (g + 1) * SSD_N)
            gs = slice(g * gw, (g + 1) * gw)
            bg = bm_ref[rows, ns]
            cg = cm_ref[rows, ns]
            cb = _dot_nt(cg, bg)
            update = _dot_tn(bg, we_s[rows, gs])
            pieces = []
            for pair in range(SSD_HPG // 2):
                cols = slice(g * gw + pair * 2 * SSD_P, g * gw + (pair + 1) * 2 * SSD_P)
                xp = xs_ref[rows, cols]
                ms = []
                for half in range(2):
                    hh = g * SSD_HPG + pair * 2 + half
                    seg = acs[b][:, hh:hh + 1] - acst[b][hh:hh + 1, :]
                    decay = jnp.exp(jnp.where(causal, seg, -1e30))
                    ms.append((cb * decay * dtt[hh:hh + 1, rows]).astype(BF16))
                xp2 = jnp.concatenate([jnp.where(low_half, xp, jnp.zeros_like(xp)),
                                       jnp.where(low_half, jnp.zeros_like(xp), xp)], axis=0)
                pieces.append(_dot(jnp.concatenate(ms, axis=1), xp2))
            y_s[rows, gs] = (_dot(cg, state[g].astype(BF16)) * ex_s[rows, gs]
                             + jnp.concatenate(pieces, axis=1))
            state[g] = state[g] * ex_s[(b + 1) * q - 1:(b + 1) * q, gs] + update
    for g in range(SSD_GROUPS):
        st_ref[g] = state[g]

    gnw = gnw_ref[...]
    dsk = dsk_ref[...]
    parts = []
    for g in range(SSD_GROUPS):
        gs = slice(g * gw, (g + 1) * gw)
        zz = z_ref[:, gs].astype(F32)
        y = (y_s[:, gs] + dsk[:, gs] * xs_ref[:, gs].astype(F32)) * (zz * _sigmoid(zz))
        parts.append((_rms(y) * gnw[:, gs]).astype(BF16))
    o_ref[...] = x_ref[...] + _dot(jnp.concatenate(parts, axis=-1), wo_ref[...])


def _ssd(x, nw, w_in, conv_w, conv_b, dt_bias, a_log, d_skip, norm_w, w_out, batch, seq_len, tm=512):
    t = x.shape[0]
    steps_per_seq = seq_len // tm
    conv_dim = SSD_DINNER + 2 * SSD_GN
    wb = w_in.astype(BF16)
    wz, wx, wdt = wb[:, :SSD_DINNER], wb[:, SSD_DINNER:SSD_DINNER + conv_dim], wb[:, SSD_DINNER + conv_dim:]
    cw = jnp.concatenate([conv_w, conv_b[None, :], jnp.zeros((3, conv_dim), F32)], axis=0)
    row = pl.BlockSpec((tm, D_MODEL), lambda i: (i, 0))

    def out_row(n):
        return pl.BlockSpec((tm, n), lambda i: (i, 0))

    z, xs, bm, cm, dt, dtt = pl.pallas_call(
        functools.partial(_ssd_proj_kernel, steps_per_seq=steps_per_seq),
        grid=(t // tm,),
        in_specs=[row, _prev_rows_spec(tm, steps_per_seq)] + [_vmem()] * 6,
        out_specs=[out_row(SSD_DINNER), out_row(SSD_DINNER), out_row(SSD_GN), out_row(SSD_GN),
                   out_row(SSD_HEADS), pl.BlockSpec((SSD_HEADS, tm), lambda i: (0, i))],
        out_shape=[jax.ShapeDtypeStruct((t, SSD_DINNER), BF16), jax.ShapeDtypeStruct((t, SSD_DINNER), BF16),
                   jax.ShapeDtypeStruct((t, SSD_GN), BF16), jax.ShapeDtypeStruct((t, SSD_GN), BF16),
                   jax.ShapeDtypeStruct((t, SSD_HEADS), F32), jax.ShapeDtypeStruct((SSD_HEADS, t), F32)],
        scratch_shapes=[pltpu.VMEM((tm + HALO, D_MODEL), BF16), pltpu.VMEM((tm + HALO, SSD_PROJ_TILE), F32),
                        pltpu.VMEM((tm + HALO, SSD_PROJ_TILE), F32)],
        compiler_params=_params("parallel"),
        name="ssd_proj",
    )(x, x, nw.reshape(1, D_MODEL), wz, wx, wdt, wdt.T, cw)

    spb = seq_len // tm
    spread = jnp.repeat(jnp.eye(SSD_HEADS, dtype=BF16), SSD_P, axis=1)
    zero = jnp.zeros_like(spread)
    top = jnp.concatenate([jnp.concatenate([spread, zero], axis=1), jnp.concatenate([zero, spread], axis=1)], axis=0)
    rexp = jnp.concatenate([top, top], axis=0)
    dsk = jnp.repeat(d_skip, SSD_P).reshape(1, SSD_DINNER)

    def blk(n):
        return pl.BlockSpec((tm, n), lambda b, i: (b * spb + i, 0))

    return pl.pallas_call(
        _ssd_core_kernel,
        grid=(batch, spb),
        in_specs=[blk(D_MODEL), blk(SSD_DINNER), blk(SSD_DINNER), blk(SSD_GN), blk(SSD_GN), blk(SSD_HEADS),
                  pl.BlockSpec((SSD_HEADS, tm), lambda b, i: (0, b * spb + i))] + [_vmem()] * 8,
        out_specs=blk(D_MODEL),
        out_shape=jax.ShapeDtypeStruct((t, D_MODEL), F32),
        scratch_shapes=[pltpu.VMEM((SSD_GROUPS, SSD_N, SSD_HPG * SSD_P), F32),
                        pltpu.VMEM((tm, SSD_DINNER), F32), pltpu.VMEM((tm, SSD_DINNER), F32),
                        pltpu.VMEM((tm, SSD_DINNER), BF16)],
        compiler_params=_params("parallel", "arbitrary"),
        name="ssd_core",
    )(x, z, xs, bm, cm, dt, dtt, dt_bias.reshape(1, SSD_HEADS), dt_bias.reshape(SSD_HEADS, 1),
      a_log.reshape(1, SSD_HEADS), a_log.reshape(SSD_HEADS, 1), dsk, norm_w.reshape(1, SSD_DINNER), rexp,
      w_out.astype(BF16))


def _sb_proj_kernel(x_ref, nw_ref, wq_ref, wk_ref, wv_ref, q_ref, k_ref, v_ref):
    h = (_rms(x_ref[...]) * nw_ref[...]).astype(BF16)
    q_ref[...] = (_dot(h, wq_ref[...]) * (SB_DH ** -0.5 * LOG2E)).astype(BF16)
    k_ref[...] = _dot(h, wk_ref[...]).astype(BF16)
    v_ref[...] = _dot(h, wv_ref[...]).astype(BF16)


def _sb_attn_kernel(q_ref, k_ref, v_ref, o_ref, acc_ref, z_s, a_s, kmax_s):
    t, nh, ns = SB_T, SB_HPS, SB_SLABS
    w = nh * SB_DH
    nc = ns * nh
    i = pl.program_id(2)
    head_of_lane = lax.broadcasted_iota(jnp.int32, (1, w), 1) // SB_DH

    def per_head(a):
        return jnp.concatenate([jnp.where(head_of_lane == h, a, jnp.zeros_like(a)) for h in range(nh)], axis=0)

    def key_rows(j):
        return pl.ds(pl.multiple_of(j * t, t), t)

    qs = [per_head(q_ref[0, :, sl * w:(sl + 1) * w]) for sl in range(ns)]
    upper = _tri(t, lower=True)
    strict = (lax.broadcasted_iota(jnp.int32, (t, t), 0) > lax.broadcasted_iota(jnp.int32, (t, t), 1))

    def logits(j, sl):
        return _dot_nt(qs[sl], k_ref[0, key_rows(j), sl * w:(sl + 1) * w])

    def survival(z_of, carry, masked, j_next):
        rcs, ys = [], []
        rows_next = key_rows(jnp.maximum(j_next, 0))
        k_next = [k_ref[0, rows_next, sl * w:(sl + 1) * w] for sl in range(ns)]
        for c in range(nc):
            z = z_of(c)
            sp = _softplus_base2(z)
            if masked:
                sp = jnp.where(strict, sp, 0.0)
            rcs.append(_dot(sp.astype(BF16), upper))
            ys.append(z - carry[c])
            sl, h = divmod(c, nh)
            z_s[c * t:(c + 1) * t, :] = _dot_nt(qs[sl][h * t:(h + 1) * t], k_next[sl])
        return rcs, ys

    def weights(rcs, ys, masked, chains):
        for c in chains:
            att = jnp.exp2(ys[c] - rcs[c])
            if masked:
                att = jnp.where(strict, att, 0.0)
            a_s[:, c * t:(c + 1) * t] = att.astype(BF16)

    def weighted_values(j):
        return [_dot(a_s[:, sl * nh * t:(sl + 1) * nh * t], per_head(v_ref[0, key_rows(j), sl * w:(sl + 1) * w]))
                for sl in range(ns)]

    zero = jnp.zeros((t, 1), F32)
    zz = [logits(i, sl) for sl in range(ns)]
    rcs, ys = survival(lambda c: zz[c // nh][(c % nh) * t:(c % nh + 1) * t], (zero,) * nc, True, i - 1)
    weights(rcs, ys, True, range(nc))
    acc_ref[...] = jnp.zeros_like(acc_ref)

    @pl.when(i == 0)
    def _():
        kmax_s[0] = jnp.max(jnp.abs(k_ref[0].astype(F32)))

    z_bound = SB_DH * jnp.max(jnp.abs(q_ref[0].astype(F32))) * kmax_s[0]
    dead = 150.0 + z_bound * (2.0 ** -7)

    def step(state):
        s, _, carry = state
        j = i - 1 - s
        pv = weighted_values(j + 1)
        rcs, ys = survival(lambda c: z_s[c * t:(c + 1) * t, :], carry, False, j - 1)
        for sl in range(ns):
            acc_ref[:, sl * w:(sl + 1) * w] += pv[sl]
        weights(rcs, ys, False, range(nc))
        carry = tuple(carry[c] + rcs[c][:, 0:1] for c in range(nc))
        lowest = carry[0]
        for c in range(1, nc):
            lowest = jnp.minimum(lowest, carry[c])
        return s + 1, jnp.min(lowest) < dead, carry

    done, _, _ = lax.while_loop(lambda state: jnp.logical_and(state[0] < i, state[1]), step,
                                (jnp.int32(0), jnp.bool_(True), tuple(rcs[c][:, 0:1] for c in range(nc))))
    pv = weighted_values(i - done)
    for sl in range(ns):
        o_ref[0, :, sl * w:(sl + 1) * w] = (acc_ref[:, sl * w:(sl + 1) * w] + pv[sl]).astype(o_ref.dtype)


def _out_proj_kernel(x_ref, a_ref, w_ref, o_ref):
    o_ref[...] = x_ref[...] + _dot(a_ref[...], w_ref[...])


def _sb(x, nw, w_qkv, w_out, batch, seq_len, tm=512):
    t = x.shape[0]
    wb = w_qkv.astype(BF16)
    row = pl.BlockSpec((tm, D_MODEL), lambda i: (i, 0))
    qkv_shape = jax.ShapeDtypeStruct((t, D_MODEL), BF16)
    q, k, v = pl.pallas_call(
        _sb_proj_kernel,
        grid=(t // tm,),
        in_specs=[row] + [_vmem()] * 4,
        out_specs=[row, row, row],
        out_shape=[qkv_shape, qkv_shape, qkv_shape],
        compiler_params=_params("parallel"),
        name="sb_proj",
    )(x, nw.reshape(1, D_MODEL), wb[:, :D_MODEL], wb[:, D_MODEL:2 * D_MODEL], wb[:, 2 * D_MODEL:])

    shape3 = (batch, seq_len, D_MODEL)
    lanes = SB_SLABS * SB_HPS * SB_DH
    chains = SB_SLABS * SB_HPS
    qblk = pl.BlockSpec((1, SB_T, lanes), lambda b, p, i: (b, i, p))
    kvblk = pl.BlockSpec((1, seq_len, lanes), lambda b, p, i: (b, 0, p))
    att = pl.pallas_call(
        _sb_attn_kernel,
        grid=(batch, SB_HEADS // chains, seq_len // SB_T),
        in_specs=[qblk, kvblk, kvblk],
        out_specs=qblk,
        out_shape=jax.ShapeDtypeStruct(shape3, BF16),
        scratch_shapes=[pltpu.VMEM((SB_T, lanes), F32), pltpu.VMEM((chains * SB_T, SB_T), F32),
                        pltpu.VMEM((SB_T, chains * SB_T), BF16), pltpu.SMEM((1,), F32)],
        compiler_params=_params("parallel", "parallel", "arbitrary"),
        name="sb_attention",
    )(q.reshape(shape3), k.reshape(shape3), v.reshape(shape3))

    return pl.pallas_call(
        _out_proj_kernel,
        grid=(t // tm,),
        in_specs=[row, row, _vmem()],
        out_specs=row,
        out_shape=jax.ShapeDtypeStruct((t, D_MODEL), F32),
        compiler_params=_params("parallel"),
        name="sb_out_proj",
    )(x, att.reshape(t, D_MODEL), w_out.astype(BF16))


def kernel(x, mix_norm_w, ffn_norm_w, final_norm_w, gla_w_in, gla_w_gate2, gla_b_gate, gla_norm_w, gla_w_out, pool_w, pool_b, pool_scale, ssd_w_in, ssd_conv_w, ssd_conv_b, ssd_dt_bias, ssd_a_log, ssd_d, ssd_norm_w, ssd_w_out, sb_w_qkv, sb_w_out, ffn_w_up, ffn_conv_w, ffn_conv_b, ffn_w_down):
    batch, seq_len, d = x.shape
    assert d == D_MODEL
    depth = mix_norm_w.shape[0]
    xf = x.reshape(batch * seq_len, d)
    for i in range(depth):
        m, j = i % 4, i // 4
        if m == 0:
            xf = _gla(xf, mix_norm_w[i], gla_w_in[j], gla_w_gate2[j], gla_b_gate[j], gla_norm_w[j], gla_w_out[j],
                      batch, seq_len)
        elif m == 1:
            xf = _pool(xf, mix_norm_w[i], pool_w[j], pool_b[j], pool_scale[j], seq_len)
        elif m == 2:
            xf = _ssd(xf, mix_norm_w[i], ssd_w_in[j], ssd_conv_w[j], ssd_conv_b[j], ssd_dt_bias[j], ssd_a_log[j],
                      ssd_d[j], ssd_norm_w[j], ssd_w_out[j], batch, seq_len)
        else:
            xf = _sb(xf, mix_norm_w[i], sb_w_qkv[j], sb_w_out[j], batch, seq_len)
        xf = _ffn(xf, ffn_norm_w[i], ffn_w_up[i], ffn_conv_w[i], ffn_conv_b[i], ffn_w_down[i], final_norm_w,
                  seq_len, final_norm=(i == depth - 1))
    return xf.reshape(batch, seq_len, d)
```

```python
import functools

import jax
import jax.numpy as jnp
from jax import lax
from jax.experimental import pallas as pl
from jax.experimental.pallas import tpu as pltpu

F32 = jnp.float32
BF16 = jnp.bfloat16

EPS = 1e-6
LOG2E = 1.4426950408889634
D_MODEL = 1024
HALO = 16
VMEM_LIMIT = 56 * 1024 * 1024

GLA_HEADS, GLA_DK, GLA_DV, GLA_RANK, GLA_TAU, GLA_CHUNK = 4, 128, 256, 16, 16.0, 64
GLA_HK, GLA_HV = GLA_HEADS * GLA_DK, GLA_HEADS * GLA_DV
POOL_WINDOWS, POOL_GW = (2, 4, 8, 16), 256
SSD_DINNER, SSD_P, SSD_HEADS, SSD_GROUPS, SSD_HPG, SSD_N, SSD_CONV = 2048, 64, 32, 4, 8, 128, 4
SSD_GN = SSD_GROUPS * SSD_N
SSD_Q = 128
SSD_PROJ_TILE = 256
SB_HEADS, SB_DH, SB_T = 16, 64, 256
SB_HPS = 4
SB_SLABS = 2
FFN_DIM, FFN_TILE, FFN_CONV = 2816, 256, 3
FFN_NT = FFN_DIM // FFN_TILE


def _params(*sem):
    return pltpu.CompilerParams(dimension_semantics=sem, vmem_limit_bytes=VMEM_LIMIT)


def _vmem():
    return pl.BlockSpec(memory_space=pltpu.VMEM)


def _rms(x):
    return x * lax.rsqrt(jnp.mean(x * x, axis=-1, keepdims=True) + EPS)


def _softplus(x):
    return jnp.maximum(x, 0.0) + jnp.log(1.0 + jnp.exp(-jnp.abs(x)))


def _softplus_base2(x):
    return jnp.where(x > 64.0, x, jnp.log(1.0 + jnp.exp2(x)) * LOG2E)


def _sigmoid(x):
    return 1.0 / (1.0 + jnp.exp(-x))


def _dot(a, b):
    return jnp.dot(a, b, preferred_element_type=F32)


def _dot_nt(a, b):
    return lax.dot_general(a, b, (((1,), (1,)), ((), ())), preferred_element_type=F32)


def _dot_tn(a, b):
    return lax.dot_general(a, b, (((0,), (0,)), ((), ())), preferred_element_type=F32)


def _split3(x):
    hi = x.astype(BF16)
    r = x - hi.astype(F32)
    mid = r.astype(BF16)
    lo = (r - mid.astype(F32)).astype(BF16)
    return hi, mid, lo


def _exact_left(m01, x):
    hi, mid, lo = _split3(x)
    return _dot(m01, lo) + _dot(m01, mid) + _dot(m01, hi)


def _exact_right(x, m01):
    hi, mid, lo = _split3(x)
    return _dot(lo, m01) + _dot(mid, m01) + _dot(hi, m01)


def _tri(n, lower):
    r = lax.broadcasted_iota(jnp.int32, (n, n), 0)
    c = lax.broadcasted_iota(jnp.int32, (n, n), 1)
    return jnp.where((r >= c) if lower else (r <= c), 1.0, 0.0).astype(BF16)


def _prev_rows_spec(tm, steps_per_seq):
    del steps_per_seq
    return pl.BlockSpec((HALO, D_MODEL), lambda i: (jnp.maximum(i * (tm // HALO) - 1, 0), 0))


def _normed_with_halo(x_ref, xp_ref, nw_ref, hs_ref, steps_per_seq):
    first = (pl.program_id(0) % steps_per_seq) == 0
    nw = nw_ref[...]
    hp = _rms(xp_ref[...]) * nw
    hs_ref[0:HALO, :] = jnp.where(first, 0.0, hp).astype(hs_ref.dtype)
    hs_ref[HALO:, :] = (_rms(x_ref[...]) * nw).astype(hs_ref.dtype)


def _ffn_kernel(x_ref, xp_ref, nw_ref, wu_ref, cw_ref, wd_ref, fw_ref, o_ref, hs_ref, a_ref,
                *, steps_per_seq, final_norm):
    _normed_with_halo(x_ref, xp_ref, nw_ref, hs_ref, steps_per_seq)

    def up(j):
        hs = hs_ref[...]
        lo = j * FFN_TILE
        return (_dot(hs, wu_ref[:, lo:lo + FFN_TILE]),
                _dot(hs, wu_ref[:, FFN_DIM + lo:FFN_DIM + lo + FFN_TILE]))

    def conv(u, lo):
        c = cw_ref[:, lo:lo + FFN_TILE]
        out = u[HALO:] * c[2:3] + c[3:4]
        out = out + pltpu.roll(u, 1, 0)[HALO:] * c[1:2]
        out = out + pltpu.roll(u, 2, 0)[HALO:] * c[0:1]
        return out

    u = up(0)
    for j in range(FFN_NT):
        nxt = up(j + 1) if j + 1 < FFN_NT else None
        g = conv(u[0], j * FFN_TILE)
        v = conv(u[1], FFN_DIM + j * FFN_TILE)
        a_ref[:, j * FFN_TILE:(j + 1) * FFN_TILE] = (g * _sigmoid(g) * v).astype(BF16)
        u = nxt
    y = x_ref[...] + _dot(a_ref[...], wd_ref[...])
    if final_norm:
        y = _rms(y) * fw_ref[...]
    o_ref[...] = y


def _ffn(x, nw, w_up, conv_w, conv_b, w_down, final_w, seq_len, final_norm, tm=1024):
    t = x.shape[0]
    steps_per_seq = seq_len // tm
    cw = jnp.concatenate([conv_w, conv_b[None, :], jnp.zeros((4, 2 * FFN_DIM), F32)], axis=0)
    row = pl.BlockSpec((tm, D_MODEL), lambda i: (i, 0))
    return pl.pallas_call(
        functools.partial(_ffn_kernel, steps_per_seq=steps_per_seq, final_norm=final_norm),
        grid=(t // tm,),
        in_specs=[row, _prev_rows_spec(tm, steps_per_seq)] + [_vmem()] * 5,
        out_specs=row,
        out_shape=jax.ShapeDtypeStruct((t, D_MODEL), F32),
        scratch_shapes=[pltpu.VMEM((tm + HALO, D_MODEL), BF16), pltpu.VMEM((tm, FFN_DIM), BF16)],
        compiler_params=_params("parallel"),
        name="conv_ffn",
    )(x, x, nw.reshape(1, D_MODEL), w_up.astype(BF16), cw, w_down.astype(BF16), final_w.reshape(1, D_MODEL))


def _gla_kernel(x_ref, nw_ref, wq_ref, wk_ref, wv_ref, wr_ref, wg1_ref, wg2_ref, bg_ref, gnw_ref, wo_ref,
                o_ref, st_ref, q_s, k_s, v_s, la_s, o_s):
    tm = x_ref.shape[0]

    @pl.when(pl.program_id(1) == 0)
    def _():
        st_ref[...] = jnp.zeros_like(st_ref)

    x = x_ref[...]
    h = (_rms(x) * nw_ref[...]).astype(BF16)
    q_s[...] = _dot(h, wq_ref[...]).astype(BF16)
    k_s[...] = _dot(h, wk_ref[...])
    v_s[...] = _dot(h, wv_ref[...]).astype(BF16)
    glr = _dot(h, wg1_ref[...]).astype(BF16)
    gate = _dot(glr, wg2_ref[...]) + bg_ref[...]
    la_s[...] = -_softplus(-gate) * (1.0 / GLA_TAU)

    tril = _tri(GLA_CHUNK, lower=True)
    scale = GLA_DK ** -0.5
    nchunks = tm // GLA_CHUNK
    k_cols = [slice(hd * GLA_DK, (hd + 1) * GLA_DK) for hd in range(GLA_HEADS)]
    v_cols = [slice(hd * GLA_DV, (hd + 1) * GLA_DV) for hd in range(GLA_HEADS)]

    def chunk_update(c):
        rows = slice(c * GLA_CHUNK, (c + 1) * GLA_CHUNK)
        g = _exact_left(tril, la_s[rows, :])
        g_end = g[GLA_CHUNK - 1:GLA_CHUNK, :]
        kd = (k_s[rows, :] * jnp.exp(g_end - g)).astype(BF16)
        vc = v_s[rows, :]
        return [_dot_tn(vc[:, v_cols[hd]], kd[:, k_cols[hd]]) for hd in range(GLA_HEADS)], jnp.exp(g_end)

    state = [st_ref[hd] for hd in range(GLA_HEADS)]
    pending = chunk_update(0)
    for c in range(nchunks):
        upd, eg = pending
        if c + 1 < nchunks:
            pending = chunk_update(c + 1)
        rows = slice(c * GLA_CHUNK, (c + 1) * GLA_CHUNK)
        qc = q_s[rows, :]
        for hd in range(GLA_HEADS):
            state[hd] = state[hd] * eg[:, k_cols[hd]] + upd[hd]
            o_s[rows, v_cols[hd]] = _dot_nt(qc[:, k_cols[hd]], state[hd].astype(BF16)) * scale
    for hd in range(GLA_HEADS):
        st_ref[hd] = state[hd]

    r = _dot(h, wr_ref[...])
    gated = r * _sigmoid(r)
    gnw = gnw_ref[...]
    parts = []
    for hd in range(GLA_HEADS):
        vs = slice(hd * GLA_DV, (hd + 1) * GLA_DV)
        parts.append((_rms(o_s[:, vs]) * gnw[:, vs] * gated[:, vs]).astype(BF16))
    o_ref[...] = x + _dot(jnp.concatenate(parts, axis=-1), wo_ref[...])


def _gla(x, nw, w_in, w_gate2, b_gate, norm_w, w_out, batch, seq_len, tm=512):
    t = x.shape[0]
    spb = seq_len // tm
    wb = w_in.astype(BF16)
    wq, wk = wb[:, :GLA_HK], wb[:, GLA_HK:2 * GLA_HK]
    wv, wr = wb[:, 2 * GLA_HK:2 * GLA_HK + GLA_HV], wb[:, 2 * GLA_HK + GLA_HV:2 * GLA_HK + 2 * GLA_HV]
    wg1 = jnp.pad(wb[:, 2 * GLA_HK + 2 * GLA_HV:], ((0, 0), (0, 128 - GLA_RANK)))
    wg2 = jnp.pad(w_gate2.astype(BF16), ((0, 128 - GLA_RANK), (0, 0)))
    row = pl.BlockSpec((tm, D_MODEL), lambda b, i: (b * spb + i, 0))
    return pl.pallas_call(
        _gla_kernel,
        grid=(batch, spb),
        in_specs=[row] + [_vmem()] * 10,
        out_specs=row,
        out_shape=jax.ShapeDtypeStruct((t, D_MODEL), F32),
        scratch_shapes=[pltpu.VMEM((GLA_HEADS, GLA_DV, GLA_DK), F32),
                        pltpu.VMEM((tm, GLA_HK), BF16), pltpu.VMEM((tm, GLA_HK), F32),
                        pltpu.VMEM((tm, GLA_HV), BF16), pltpu.VMEM((tm, GLA_HK), F32),
                        pltpu.VMEM((tm, GLA_HV), F32)],
        compiler_params=_params("parallel", "arbitrary"),
        name="gla_mixer",
    )(x, nw.reshape(1, D_MODEL), wq, wk, wv, wr, wg1, wg2, b_gate.reshape(1, GLA_HK),
      norm_w.reshape(1, GLA_HV), w_out.astype(BF16))


def _pool_kernel(x_ref, xp_ref, nw_ref, w_ref, b_ref, sc_ref, o_ref, hs_ref, *, steps_per_seq):
    tm = x_ref.shape[0]
    _normed_with_halo(x_ref, xp_ref, nw_ref, hs_ref, steps_per_seq)
    pos = (pl.program_id(0) % steps_per_seq) * tm + lax.broadcasted_iota(jnp.int32, (tm, 1), 0)
    outs = []
    for gi, win in enumerate(POOL_WINDOWS):
        cols = slice(gi * POOL_GW, (gi + 1) * POOL_GW)
        cur = hs_ref[HALO:, cols]
        total = cur
        for back in range(1, win):
            total = total + hs_ref[HALO - back:HALO - back + tm, cols]
        count = jnp.minimum(pos + 1, win).astype(F32)
        dlt = (total / count - cur).astype(BF16)
        outs.append(_dot(dlt, w_ref[gi]) + b_ref[gi])
    o_ref[...] = x_ref[...] + jnp.concatenate(outs, axis=-1) * sc_ref[...]


def _pool(x, nw, w_grp, b_grp, scale, seq_len, tm=512):
    t = x.shape[0]
    steps_per_seq = seq_len // tm
    row = pl.BlockSpec((tm, D_MODEL), lambda i: (i, 0))
    return pl.pallas_call(
        functools.partial(_pool_kernel, steps_per_seq=steps_per_seq),
        grid=(t // tm,),
        in_specs=[row, _prev_rows_spec(tm, steps_per_seq), _vmem(), _vmem(), _vmem(), _vmem()],
        out_specs=row,
        out_shape=jax.ShapeDtypeStruct((t, D_MODEL), F32),
        scratch_shapes=[pltpu.VMEM((tm + HALO, D_MODEL), F32)],
        compiler_params=_params("parallel"),
        name="pool_mixer",
    )(x, x, nw.reshape(1, D_MODEL), w_grp.astype(BF16), b_grp.reshape(len(POOL_WINDOWS), 1, POOL_GW),
      scale.reshape(1, D_MODEL))


def _ssd_proj_kernel(x_ref, xp_ref, nw_ref, wz_ref, wx_ref, wdt_ref, wdtt_ref, cw_ref, z_ref, xs_ref, bm_ref,
                     cm_ref, dt_ref, dtt_ref, hs_ref, u0_ref, u1_ref, *, steps_per_seq):
    tm = x_ref.shape[0]
    _normed_with_halo(x_ref, xp_ref, nw_ref, hs_ref, steps_per_seq)
    h = hs_ref[HALO:, :]
    tile, ztile = SSD_PROJ_TILE, 2 * SSD_PROJ_TILE
    ntiles = (SSD_DINNER + 2 * SSD_GN) // tile
    bufs = (u0_ref, u1_ref)

    def up(j):
        bufs[j % 2][...] = _dot(hs_ref[...], wx_ref[:, j * tile:(j + 1) * tile])

    def conv(j):
        lo = j * tile
        c = cw_ref[:, lo:lo + tile]
        if lo < SSD_DINNER:
            dst, off = xs_ref, lo
        elif lo < SSD_DINNER + SSD_GN:
            dst, off = bm_ref, lo - SSD_DINNER
        else:
            dst, off = cm_ref, lo - SSD_DINNER - SSD_GN
        src = bufs[j % 2]
        acc = src[HALO:, :] * c[3:4] + c[4:5]
        for back in range(1, SSD_CONV):
            acc = acc + src[HALO - back:HALO - back + tm, :] * c[3 - back:4 - back]
        dst[:, off:off + tile] = (acc * _sigmoid(acc)).astype(BF16)

    up(0)
    for j in range(ntiles):
        if j + 1 < ntiles:
            up(j + 1)
        if j % 3 == 2:
            zc = slice((j // 3) * ztile, (j // 3 + 1) * ztile)
            z_ref[:, zc] = _dot(h, wz_ref[:, zc]).astype(z_ref.dtype)
        conv(j)
    dt_ref[...] = _dot(h, wdt_ref[...])
    dtt_ref[...] = _dot_nt(wdtt_ref[...], h)


def _ssd_core_kernel(x_ref, z_ref, xs_ref, bm_ref, cm_ref, dt_ref, dtt_ref, dtb_r, dtb_c, alog_r, alog_c,
                     dsk_ref, gnw_ref, rexp_ref, wo_ref, o_ref, st_ref, y_s, ex_s, we_s):
    tm, q = x_ref.shape[0], SSD_Q
    gw = SSD_HPG * SSD_P

    @pl.when(pl.program_id(1) == 0)
    def _():
        st_ref[...] = jnp.zeros_like(st_ref)

    dt = _softplus(dt_ref[...] + dtb_r[...])
    dtt = _softplus(dtt_ref[...] + dtb_c[...])
    a = dt * -jnp.exp(alog_r[...])
    at = dtt * -jnp.exp(alog_c[...])
    tri_l, tri_u = _tri(q, lower=True), _tri(q, lower=False)
    acs, acst, per_row = [], [], []
    for b in range(tm // q):
        rows = slice(b * q, (b + 1) * q)
        acs_b = _exact_left(tri_l, a[rows])
        acs.append(acs_b)
        acst.append(_exact_right(at[:, rows], tri_u))
        per_row.append(jnp.concatenate([jnp.exp(acs_b), dt[rows] * jnp.exp(acs_b[q - 1:q] - acs_b)], axis=1))
    ew = jnp.concatenate(per_row, axis=0)
    ew_hi = ew.astype(BF16)
    ew_parts = jnp.concatenate([ew_hi, (ew - ew_hi.astype(F32)).astype(BF16)], axis=1)
    ex_s[...] = _dot(ew_parts, rexp_ref[:, :SSD_DINNER])
    we_s[...] = (xs_ref[...].astype(F32) * _dot(ew_parts, rexp_ref[:, SSD_DINNER:])).astype(BF16)

    causal = (lax.broadcasted_iota(jnp.int32, (q, q), 0) >= lax.broadcasted_iota(jnp.int32, (q, q), 1))
    low_half = lax.broadcasted_iota(jnp.int32, (1, 2 * SSD_P), 1) < SSD_P
    state = [st_ref[g] for g in range(SSD_GROUPS)]
    for b in range(tm // q):
        rows = slice(b * q, (b + 1) * q)
        for g in range(SSD_GROUPS):
            ns = slice(g * SSD_N, (g + 1) * SSD_N)
            gs = slice(g * gw, (g + 1) * gw)
            bg = bm_ref[rows, ns]
            cg = cm_ref[rows, ns]
            cb = _dot_nt(cg, bg)
            update = _dot_tn(bg, we_s[rows, gs])
            pieces = []
            for pair in range(SSD_HPG // 2):
                cols = slice(g * gw + pair * 2 * SSD_P, g * gw + (pair + 1) * 2 * SSD_P)
                xp = xs_ref[rows, cols]
                ms = []
                for half in range(2):
                    hh = g * SSD_HPG + pair * 2 + half
                    seg = acs[b][:, hh:hh + 1] - acst[b][hh:hh + 1, :]
                    decay = jnp.exp(jnp.where(causal, seg, -1e30))
                    ms.append((cb * decay * dtt[hh:hh + 1, rows]).astype(BF16))
                xp2 = jnp.concatenate([jnp.where(low_half, xp, jnp.zeros_like(xp)),
                                       jnp.where(low_half, jnp.zeros_like(xp), xp)], axis=0)
                pieces.append(_dot(jnp.concatenate(ms, axis=1), xp2))
            y_s[rows, gs] = (_dot(cg, state[g].astype(BF16)) * ex_s[rows, gs]
                             + jnp.concatenate(pieces, axis=1))
            state[g] = state[g] * ex_s[(b + 1) * q - 1:(b + 1) * q, gs] + update
    for g in range(SSD_GROUPS):
        st_ref[g] = state[g]

    gnw = gnw_ref[...]
    dsk = dsk_ref[...]
    parts = []
    for g in range(SSD_GROUPS):
        gs = slice(g * gw, (g + 1) * gw)
        zz = z_ref[:, gs].astype(F32)
        y = (y_s[:, gs] + dsk[:, gs] * xs_ref[:, gs].astype(F32)) * (zz * _sigmoid(zz))
        parts.append((_rms(y) * gnw[:, gs]).astype(BF16))
    o_ref[...] = x_ref[...] + _dot(jnp.concatenate(parts, axis=-1), wo_ref[...])


def _ssd(x, nw, w_in, conv_w, conv_b, dt_bias, a_log, d_skip, norm_w, w_out, batch, seq_len, tm=512):
    t = x.shape[0]
    steps_per_seq = seq_len // tm
    conv_dim = SSD_DINNER + 2 * SSD_GN
    wb = w_in.astype(BF16)
    wz, wx, wdt = wb[:, :SSD_DINNER], wb[:, SSD_DINNER:SSD_DINNER + conv_dim], wb[:, SSD_DINNER + conv_dim:]
    cw = jnp.concatenate([conv_w, conv_b[None, :], jnp.zeros((3, conv_dim), F32)], axis=0)
    row = pl.BlockSpec((tm, D_MODEL), lambda i: (i, 0))

    def out_row(n):
        return pl.BlockSpec((tm, n), lambda i: (i, 0))

    z, xs, bm, cm, dt, dtt = pl.pallas_call(
        functools.partial(_ssd_proj_kernel, steps_per_seq=steps_per_seq),
        grid=(t // tm,),
        in_specs=[row, _prev_rows_spec(tm, steps_per_seq)] + [_vmem()] * 6,
        out_specs=[out_row(SSD_DINNER), out_row(SSD_DINNER), out_row(SSD_GN), out_row(SSD_GN),
                   out_row(SSD_HEADS), pl.BlockSpec((SSD_HEADS, tm), lambda i: (0, i))],
        out_shape=[jax.ShapeDtypeStruct((t, SSD_DINNER), BF16), jax.ShapeDtypeStruct((t, SSD_DINNER), BF16),
                   jax.ShapeDtypeStruct((t, SSD_GN), BF16), jax.ShapeDtypeStruct((t, SSD_GN), BF16),
                   jax.ShapeDtypeStruct((t, SSD_HEADS), F32), jax.ShapeDtypeStruct((SSD_HEADS, t), F32)],
        scratch_shapes=[pltpu.VMEM((tm + HALO, D_MODEL), BF16), pltpu.VMEM((tm + HALO, SSD_PROJ_TILE), F32),
                        pltpu.VMEM((tm + HALO, SSD_PROJ_TILE), F32)],
        compiler_params=_params("parallel"),
        name="ssd_proj",
    )(x, x, nw.reshape(1, D_MODEL), wz, wx, wdt, wdt.T, cw)

    spb = seq_len // tm
    spread = jnp.repeat(jnp.eye(SSD_HEADS, dtype=BF16), SSD_P, axis=1)
    zero = jnp.zeros_like(spread)
    top = jnp.concatenate([jnp.concatenate([spread, zero], axis=1), jnp.concatenate([zero, spread], axis=1)], axis=0)
    rexp = jnp.concatenate([top, top], axis=0)
    dsk = jnp.repeat(d_skip, SSD_P).reshape(1, SSD_DINNER)

    def blk(n):
        return pl.BlockSpec((tm, n), lambda b, i: (b * spb + i, 0))

    return pl.pallas_call(
        _ssd_core_kernel,
        grid=(batch, spb),
        in_specs=[blk(D_MODEL), blk(SSD_DINNER), blk(SSD_DINNER), blk(SSD_GN), blk(SSD_GN), blk(SSD_HEADS),
                  pl.BlockSpec((SSD_HEADS, tm), lambda b, i: (0, b * spb + i))] + [_vmem()] * 8,
        out_specs=blk(D_MODEL),
        out_shape=jax.ShapeDtypeStruct((t, D_MODEL), F32),
        scratch_shapes=[pltpu.VMEM((SSD_GROUPS, SSD_N, SSD_HPG * SSD_P), F32),
                        pltpu.VMEM((tm, SSD_DINNER), F32), pltpu.VMEM((tm, SSD_DINNER), F32),
                        pltpu.VMEM((tm, SSD_DINNER), BF16)],
        compiler_params=_params("parallel", "arbitrary"),
        name="ssd_core",
    )(x, z, xs, bm, cm, dt, dtt, dt_bias.reshape(1, SSD_HEADS), dt_bias.reshape(SSD_HEADS, 1),
      a_log.reshape(1, SSD_HEADS), a_log.reshape(SSD_HEADS, 1), dsk, norm_w.reshape(1, SSD_DINNER), rexp,
      w_out.astype(BF16))


def _sb_proj_kernel(x_ref, nw_ref, wq_ref, wk_ref, wv_ref, q_ref, k_ref, v_ref):
    h = (_rms(x_ref[...]) * nw_ref[...]).astype(BF16)
    q_ref[...] = (_dot(h, wq_ref[...]) * (SB_DH ** -0.5 * LOG2E)).astype(BF16)
    k_ref[...] = _dot(h, wk_ref[...]).astype(BF16)
    v_ref[...] = _dot(h, wv_ref[...]).astype(BF16)


def _sb_attn_kernel(q_ref, k_ref, v_ref, o_ref, acc_ref, z_s, a_s, kmax_s):
    t, nh, ns = SB_T, SB_HPS, SB_SLABS
    w = nh * SB_DH
    nc = ns * nh
    i = pl.program_id(2)
    head_of_lane = lax.broadcasted_iota(jnp.int32, (1, w), 1) // SB_DH

    def per_head(a):
        return jnp.concatenate([jnp.where(head_of_lane == h, a, jnp.zeros_like(a)) for h in range(nh)], axis=0)

    def key_rows(j):
        return pl.ds(pl.multiple_of(j * t, t), t)

    qs = [per_head(q_ref[0, :, sl * w:(sl + 1) * w]) for sl in range(ns)]
    upper = _tri(t, lower=True)
    strict = (lax.broadcasted_iota(jnp.int32, (t, t), 0) > lax.broadcasted_iota(jnp.int32, (t, t), 1))

    def logits(j, sl):
        return _dot_nt(qs[sl], k_ref[0, key_rows(j), sl * w:(sl + 1) * w])

    def survival(z_of, carry, masked, j_next):
        rcs, ys = [], []
        rows_next = key_rows(jnp.maximum(j_next, 0))
        k_next = [k_ref[0, rows_next, sl * w:(sl + 1) * w] for sl in range(ns)]
        for c in range(nc):
            z = z_of(c)
            sp = _softplus_base2(z)
            if masked:
                sp = jnp.where(strict, sp, 0.0)
            rcs.append(_dot(sp.astype(BF16), upper))
            ys.append(z - carry[c])
            sl, h = divmod(c, nh)
            z_s[c * t:(c + 1) * t, :] = _dot_nt(qs[sl][h * t:(h + 1) * t], k_next[sl])
        return rcs, ys

    def weights(rcs, ys, masked, chains):
        for c in chains:
            att = jnp.exp2(ys[c] - rcs[c])
            if masked:
                att = jnp.where(strict, att, 0.0)
            a_s[:, c * t:(c + 1) * t] = att.astype(BF16)

    def weighted_values(j):
        return [_dot(a_s[:, sl * nh * t:(sl + 1) * nh * t], per_head(v_ref[0, key_rows(j), sl * w:(sl + 1) * w]))
                for sl in range(ns)]

    zero = jnp.zeros((t, 1), F32)
    zz = [logits(i, sl) for sl in range(ns)]
    rcs, ys = survival(lambda c: zz[c // nh][(c % nh) * t:(c % nh + 1) * t], (zero,) * nc, True, i - 1)
    weights(rcs, ys, True, range(nc))
    acc_ref[...] = jnp.zeros_like(acc_ref)

    @pl.when(i == 0)
    def _():
        kmax_s[0] = jnp.max(jnp.abs(k_ref[0].astype(F32)))

    z_bound = SB_DH * jnp.max(jnp.abs(q_ref[0].astype(F32))) * kmax_s[0]
    dead = 150.0 + z_bound * (2.0 ** -7)

    def step(state):
        s, _, carry = state
        j = i - 1 - s
        pv = weighted_values(j + 1)
        rcs, ys = survival(lambda c: z_s[c * t:(c + 1) * t, :], carry, False, j - 1)
        for sl in range(ns):
            acc_ref[:, sl * w:(sl + 1) * w] += pv[sl]
        weights(rcs, ys, False, range(nc))
        carry = tuple(carry[c] + rcs[c][:, 0:1] for c in range(nc))
        lowest = carry[0]
        for c in range(1, nc):
            lowest = jnp.minimum(lowest, carry[c])
        return s + 1, jnp.min(lowest) < dead, carry

    done, _, _ = lax.while_loop(lambda state: jnp.logical_and(state[0] < i, state[1]), step,
                                (jnp.int32(0), jnp.bool_(True), tuple(rcs[c][:, 0:1] for c in range(nc))))
    pv = weighted_values(i - done)
    for sl in range(ns):
        o_ref[0, :, sl * w:(sl + 1) * w] = (acc_ref[:, sl * w:(sl + 1) * w] + pv[sl]).astype(o_ref.dtype)


def _out_proj_kernel(x_ref, a_ref, w_ref, o_ref):
    o_ref[...] = x_ref[...] + _dot(a_ref[...], w_ref[...])


def _sb(x, nw, w_qkv, w_out, batch, seq_len, tm=512):
    t = x.shape[0]
    wb = w_qkv.astype(BF16)
    row = pl.BlockSpec((tm, D_MODEL), lambda i: (i, 0))
    qkv_shape = jax.ShapeDtypeStruct((t, D_MODEL), BF16)
    q, k, v = pl.pallas_call(
        _sb_proj_kernel,
        grid=(t // tm,),
        in_specs=[row] + [_vmem()] * 4,
        out_specs=[row, row, row],
        out_shape=[qkv_shape, qkv_shape, qkv_shape],
        compiler_params=_params("parallel"),
        name="sb_proj",
    )(x, nw.reshape(1, D_MODEL), wb[:, :D_MODEL], wb[:, D_MODEL:2 * D_MODEL], wb[:, 2 * D_MODEL:])

    shape3 = (batch, seq_len, D_MODEL)
    lanes = SB_SLABS * SB_HPS * SB_DH
    chains = SB_SLABS * SB_HPS
    qblk = pl.BlockSpec((1, SB_T, lanes), lambda b, p, i: (b, i, p))
    kvblk = pl.BlockSpec((1, seq_len, lanes), lambda b, p, i: (b, 0, p))
    att = pl.pallas_call(
        _sb_attn_kernel,
        grid=(batch, SB_HEADS // chains, seq_len // SB_T),
        in_specs=[qblk, kvblk, kvblk],
        out_specs=qblk,
        out_shape=jax.ShapeDtypeStruct(shape3, BF16),
        scratch_shapes=[pltpu.VMEM((SB_T, lanes), F32), pltpu.VMEM((chains * SB_T, SB_T), F32),
                        pltpu.VMEM((SB_T, chains * SB_T), BF16), pltpu.SMEM((1,), F32)],
        compiler_params=_params("parallel", "parallel", "arbitrary"),
        name="sb_attention",
    )(q.reshape(shape3), k.reshape(shape3), v.reshape(shape3))

    return pl.pallas_call(
        _out_proj_kernel,
        grid=(t // tm,),
        in_specs=[row, row, _vmem()],
        out_specs=row,
        out_shape=jax.ShapeDtypeStruct((t, D_MODEL), F32),
        compiler_params=_params("parallel"),
        name="sb_out_proj",
    )(x, att.reshape(t, D_MODEL), w_out.astype(BF16))


def kernel(x, mix_norm_w, ffn_norm_w, final_norm_w, gla_w_in, gla_w_gate2, gla_b_gate, gla_norm_w, gla_w_out, pool_w, pool_b, pool_scale, ssd_w_in, ssd_conv_w, ssd_conv_b, ssd_dt_bias, ssd_a_log, ssd_d, ssd_norm_w, ssd_w_out, sb_w_qkv, sb_w_out, ffn_w_up, ffn_conv_w, ffn_conv_b, ffn_w_down):
    batch, seq_len, d = x.shape
    assert d == D_MODEL
    depth = mix_norm_w.shape[0]
    xf = x.reshape(batch * seq_len, d)
    for i in range(depth):
        m, j = i % 4, i // 4
        if m == 0:
            xf = _gla(xf, mix_norm_w[i], gla_w_in[j], gla_w_gate2[j], gla_b_gate[j], gla_norm_w[j], gla_w_out[j],
                      batch, seq_len)
        elif m == 1:
            xf = _pool(xf, mix_norm_w[i], pool_w[j], pool_b[j], pool_scale[j], seq_len)
        elif m == 2:
            xf = _ssd(xf, mix_norm_w[i], ssd_w_in[j], ssd_conv_w[j], ssd_conv_b[j], ssd_dt_bias[j], ssd_a_log[j],
                      ssd_d[j], ssd_norm_w[j], ssd_w_out[j], batch, seq_len)
        else:
            xf = _sb(xf, mix_norm_w[i], sb_w_qkv[j], sb_w_out[j], batch, seq_len)
        xf = _ffn(xf, ffn_norm_w[i], ffn_w_up[i], ffn_conv_w[i], ffn_conv_b[i], ffn_w_down[i], final_norm_w,
                  seq_len, final_norm=(i == depth - 1))
    return xf.reshape(batch, seq_len, d)
```

```python
import functools

import jax
import jax.numpy as jnp
from jax import lax
from jax.experimental import pallas as pl
from jax.experimental.pallas import tpu as pltpu

F32 = jnp.float32
BF16 = jnp.bfloat16

EPS = 1e-6
LOG2E = 1.4426950408889634
D_MODEL = 1024
HALO = 16
VMEM_LIMIT = 56 * 1024 * 1024

GLA_HEADS, GLA_DK, GLA_DV, GLA_RANK, GLA_TAU, GLA_CHUNK = 4, 128, 256, 16, 16.0, 64
GLA_HK, GLA_HV = GLA_HEADS * GLA_DK, GLA_HEADS * GLA_DV
POOL_WINDOWS, POOL_GW = (2, 4, 8, 16), 256
SSD_DINNER, SSD_P, SSD_HEADS, SSD_GROUPS, SSD_HPG, SSD_N, SSD_CONV = 2048, 64, 32, 4, 8, 128, 4
SSD_GN = SSD_GROUPS * SSD_N
SSD_Q = 128
SSD_PROJ_TILE = 256
SB_HEADS, SB_DH, SB_T = 16, 64, 256
SB_HPS = 4
SB_SLABS = 4
FFN_DIM, FFN_TILE, FFN_CONV = 2816, 256, 3
FFN_NT = FFN_DIM // FFN_TILE


def _params(*sem):
    return pltpu.CompilerParams(dimension_semantics=sem, vmem_limit_bytes=VMEM_LIMIT)


def _vmem():
    return pl.BlockSpec(memory_space=pltpu.VMEM)


def _rms(x):
    return x * lax.rsqrt(jnp.mean(x * x, axis=-1, keepdims=True) + EPS)


def _softplus(x):
    return jnp.maximum(x, 0.0) + jnp.log(1.0 + jnp.exp(-jnp.abs(x)))


def _softplus_base2(x):
    return jnp.where(x > 64.0, x, jnp.log(1.0 + jnp.exp2(x)) * LOG2E)


def _sigmoid(x):
    return 1.0 / (1.0 + jnp.exp(-x))


def _dot(a, b):
    return jnp.dot(a, b, preferred_element_type=F32)


def _dot_nt(a, b):
    return lax.dot_general(a, b, (((1,), (1,)), ((), ())), preferred_element_type=F32)


def _dot_tn(a, b):
    return lax.dot_general(a, b, (((0,), (0,)), ((), ())), preferred_element_type=F32)


def _split3(x):
    hi = x.astype(BF16)
    r = x - hi.astype(F32)
    mid = r.astype(BF16)
    lo = (r - mid.astype(F32)).astype(BF16)
    return hi, mid, lo


def _exact_left(m01, x):
    hi, mid, lo = _split3(x)
    return _dot(m01, lo) + _dot(m01, mid) + _dot(m01, hi)


def _exact_right(x, m01):
    hi, mid, lo = _split3(x)
    return _dot(lo, m01) + _dot(mid, m01) + _dot(hi, m01)


def _tri(n, lower):
    r = lax.broadcasted_iota(jnp.int32, (n, n), 0)
    c = lax.broadcasted_iota(jnp.int32, (n, n), 1)
    return jnp.where((r >= c) if lower else (r <= c), 1.0, 0.0).astype(BF16)


def _prev_rows_spec(tm, steps_per_seq):
    del steps_per_seq
    return pl.BlockSpec((HALO, D_MODEL), lambda i: (jnp.maximum(i * (tm // HALO) - 1, 0), 0))


def _normed_with_halo(x_ref, xp_ref, nw_ref, hs_ref, steps_per_seq):
    first = (pl.program_id(0) % steps_per_seq) == 0
    nw = nw_ref[...]
    hp = _rms(xp_ref[...]) * nw
    hs_ref[0:HALO, :] = jnp.where(first, 0.0, hp).astype(hs_ref.dtype)
    hs_ref[HALO:, :] = (_rms(x_ref[...]) * nw).astype(hs_ref.dtype)


def _ffn_kernel(x_ref, xp_ref, nw_ref, wu_ref, cw_ref, wd_ref, fw_ref, o_ref, hs_ref, a_ref,
                *, steps_per_seq, final_norm):
    _normed_with_halo(x_ref, xp_ref, nw_ref, hs_ref, steps_per_seq)

    def up(j):
        hs = hs_ref[...]
        lo = j * FFN_TILE
        return (_dot(hs, wu_ref[:, lo:lo + FFN_TILE]),
                _dot(hs, wu_ref[:, FFN_DIM + lo:FFN_DIM + lo + FFN_TILE]))

    def conv(u, lo):
        c = cw_ref[:, lo:lo + FFN_TILE]
        out = u[HALO:] * c[2:3] + c[3:4]
        out = out + pltpu.roll(u, 1, 0)[HALO:] * c[1:2]
        out = out + pltpu.roll(u, 2, 0)[HALO:] * c[0:1]
        return out

    u = up(0)
    for j in range(FFN_NT):
        nxt = up(j + 1) if j + 1 < FFN_NT else None
        g = conv(u[0], j * FFN_TILE)
        v = conv(u[1], FFN_DIM + j * FFN_TILE)
        a_ref[:, j * FFN_TILE:(j + 1) * FFN_TILE] = (g * _sigmoid(g) * v).astype(BF16)
        u = nxt
    y = x_ref[...] + _dot(a_ref[...], wd_ref[...])
    if final_norm:
        y = _rms(y) * fw_ref[...]
    o_ref[...] = y


def _ffn(x, nw, w_up, conv_w, conv_b, w_down, final_w, seq_len, final_norm, tm=1024):
    t = x.shape[0]
    steps_per_seq = seq_len // tm
    cw = jnp.concatenate([conv_w, conv_b[None, :], jnp.zeros((4, 2 * FFN_DIM), F32)], axis=0)
    row = pl.BlockSpec((tm, D_MODEL), lambda i: (i, 0))
    return pl.pallas_call(
        functools.partial(_ffn_kernel, steps_per_seq=steps_per_seq, final_norm=final_norm),
        grid=(t // tm,),
        in_specs=[row, _prev_rows_spec(tm, steps_per_seq)] + [_vmem()] * 5,
        out_specs=row,
        out_shape=jax.ShapeDtypeStruct((t, D_MODEL), F32),
        scratch_shapes=[pltpu.VMEM((tm + HALO, D_MODEL), BF16), pltpu.VMEM((tm, FFN_DIM), BF16)],
        compiler_params=_params("parallel"),
        name="conv_ffn",
    )(x, x, nw.reshape(1, D_MODEL), w_up.astype(BF16), cw, w_down.astype(BF16), final_w.reshape(1, D_MODEL))


def _gla_kernel(x_ref, nw_ref, wq_ref, wk_ref, wv_ref, wr_ref, wg1_ref, wg2_ref, bg_ref, gnw_ref, wo_ref,
                o_ref, st_ref, q_s, k_s, v_s, la_s, o_s):
    tm = x_ref.shape[0]

    @pl.when(pl.program_id(1) == 0)
    def _():
        st_ref[...] = jnp.zeros_like(st_ref)

    x = x_ref[...]
    h = (_rms(x) * nw_ref[...]).astype(BF16)
    q_s[...] = _dot(h, wq_ref[...]).astype(BF16)
    k_s[...] = _dot(h, wk_ref[...])
    v_s[...] = _dot(h, wv_ref[...]).astype(BF16)
    glr = _dot(h, wg1_ref[...]).astype(BF16)
    gate = _dot(glr, wg2_ref[...]) + bg_ref[...]
    la_s[...] = -_softplus(-gate) * (1.0 / GLA_TAU)

    tril = _tri(GLA_CHUNK, lower=True)
    scale = GLA_DK ** -0.5
    nchunks = tm // GLA_CHUNK
    k_cols = [slice(hd * GLA_DK, (hd + 1) * GLA_DK) for hd in range(GLA_HEADS)]
    v_cols = [slice(hd * GLA_DV, (hd + 1) * GLA_DV) for hd in range(GLA_HEADS)]

    def chunk_update(c):
        rows = slice(c * GLA_CHUNK, (c + 1) * GLA_CHUNK)
        g = _exact_left(tril, la_s[rows, :])
        g_end = g[GLA_CHUNK - 1:GLA_CHUNK, :]
        kd = (k_s[rows, :] * jnp.exp(g_end - g)).astype(BF16)
        vc = v_s[rows, :]
        return [_dot_tn(vc[:, v_cols[hd]], kd[:, k_cols[hd]]) for hd in range(GLA_HEADS)], jnp.exp(g_end)

    state = [st_ref[hd] for hd in range(GLA_HEADS)]
    pending = chunk_update(0)
    for c in range(nchunks):
        upd, eg = pending
        if c + 1 < nchunks:
            pending = chunk_update(c + 1)
        rows = slice(c * GLA_CHUNK, (c + 1) * GLA_CHUNK)
        qc = q_s[rows, :]
        for hd in range(GLA_HEADS):
            state[hd] = state[hd] * eg[:, k_cols[hd]] + upd[hd]
            o_s[rows, v_cols[hd]] = _dot_nt(qc[:, k_cols[hd]], state[hd].astype(BF16)) * scale
    for hd in range(GLA_HEADS):
        st_ref[hd] = state[hd]

    r = _dot(h, wr_ref[...])
    gated = r * _sigmoid(r)
    gnw = gnw_ref[...]
    parts = []
    for hd in range(GLA_HEADS):
        vs = slice(hd * GLA_DV, (hd + 1) * GLA_DV)
        parts.append((_rms(o_s[:, vs]) * gnw[:, vs] * gated[:, vs]).astype(BF16))
    o_ref[...] = x + _dot(jnp.concatenate(parts, axis=-1), wo_ref[...])


def _gla(x, nw, w_in, w_gate2, b_gate, norm_w, w_out, batch, seq_len, tm=1024):
    t = x.shape[0]
    spb = seq_len // tm
    wb = w_in.astype(BF16)
    wq, wk = wb[:, :GLA_HK], wb[:, GLA_HK:2 * GLA_HK]
    wv, wr = wb[:, 2 * GLA_HK:2 * GLA_HK + GLA_HV], wb[:, 2 * GLA_HK + GLA_HV:2 * GLA_HK + 2 * GLA_HV]
    wg1 = jnp.pad(wb[:, 2 * GLA_HK + 2 * GLA_HV:], ((0, 0), (0, 128 - GLA_RANK)))
    wg2 = jnp.pad(w_gate2.astype(BF16), ((0, 128 - GLA_RANK), (0, 0)))
    row = pl.BlockSpec((tm, D_MODEL), lambda b, i: (b * spb + i, 0))
    return pl.pallas_call(
        _gla_kernel,
        grid=(batch, spb),
        in_specs=[row] + [_vmem()] * 10,
        out_specs=row,
        out_shape=jax.ShapeDtypeStruct((t, D_MODEL), F32),
        scratch_shapes=[pltpu.VMEM((GLA_HEADS, GLA_DV, GLA_DK), F32),
                        pltpu.VMEM((tm, GLA_HK), BF16), pltpu.VMEM((tm, GLA_HK), F32),
                        pltpu.VMEM((tm, GLA_HV), BF16), pltpu.VMEM((tm, GLA_HK), F32),
                        pltpu.VMEM((tm, GLA_HV), F32)],
        compiler_params=_params("parallel", "arbitrary"),
        name="gla_mixer",
    )(x, nw.reshape(1, D_MODEL), wq, wk, wv, wr, wg1, wg2, b_gate.reshape(1, GLA_HK),
      norm_w.reshape(1, GLA_HV), w_out.astype(BF16))


def _pool_kernel(x_ref, xp_ref, nw_ref, w_ref, b_ref, sc_ref, o_ref, hs_ref, *, steps_per_seq):
    tm = x_ref.shape[0]
    _normed_with_halo(x_ref, xp_ref, nw_ref, hs_ref, steps_per_seq)
    pos = (pl.program_id(0) % steps_per_seq) * tm + lax.broadcasted_iota(jnp.int32, (tm, 1), 0)
    outs = []
    for gi, win in enumerate(POOL_WINDOWS):
        cols = slice(gi * POOL_GW, (gi + 1) * POOL_GW)
        cur = hs_ref[HALO:, cols]
        total = cur
        for back in range(1, win):
            total = total + hs_ref[HALO - back:HALO - back + tm, cols]
        count = jnp.minimum(pos + 1, win).astype(F32)
        dlt = (total / count - cur).astype(BF16)
        outs.append(_dot(dlt, w_ref[gi]) + b_ref[gi])
    o_ref[...] = x_ref[...] + jnp.concatenate(outs, axis=-1) * sc_ref[...]


def _pool(x, nw, w_grp, b_grp, scale, seq_len, tm=512):
    t = x.shape[0]
    steps_per_seq = seq_len // tm
    row = pl.BlockSpec((tm, D_MODEL), lambda i: (i, 0))
    return pl.pallas_call(
        functools.partial(_pool_kernel, steps_per_seq=steps_per_seq),
        grid=(t // tm,),
        in_specs=[row, _prev_rows_spec(tm, steps_per_seq), _vmem(), _vmem(), _vmem(), _vmem()],
        out_specs=row,
        out_shape=jax.ShapeDtypeStruct((t, D_MODEL), F32),
        scratch_shapes=[pltpu.VMEM((tm + HALO, D_MODEL), F32)],
        compiler_params=_params("parallel"),
        name="pool_mixer",
    )(x, x, nw.reshape(1, D_MODEL), w_grp.astype(BF16), b_grp.reshape(len(POOL_WINDOWS), 1, POOL_GW),
      scale.reshape(1, D_MODEL))


def _ssd_proj_kernel(x_ref, xp_ref, nw_ref, wz_ref, wx_ref, wdt_ref, wdtt_ref, cw_ref, z_ref, xs_ref, bm_ref,
                     cm_ref, dt_ref, dtt_ref, hs_ref, u0_ref, u1_ref, *, steps_per_seq):
    tm = x_ref.shape[0]
    _normed_with_halo(x_ref, xp_ref, nw_ref, hs_ref, steps_per_seq)
    h = hs_ref[HALO:, :]
    tile, ztile = SSD_PROJ_TILE, 2 * SSD_PROJ_TILE
    ntiles = (SSD_DINNER + 2 * SSD_GN) // tile
    bufs = (u0_ref, u1_ref)

    def up(j):
        bufs[j % 2][...] = _dot(hs_ref[...], wx_ref[:, j * tile:(j + 1) * tile])

    def conv(j):
        lo = j * tile
        c = cw_ref[:, lo:lo + tile]
        if lo < SSD_DINNER:
            dst, off = xs_ref, lo
        elif lo < SSD_DINNER + SSD_GN:
            dst, off = bm_ref, lo - SSD_DINNER
        else:
            dst, off = cm_ref, lo - SSD_DINNER - SSD_GN
        src = bufs[j % 2]
        acc = src[HALO:, :] * c[3:4] + c[4:5]
        for back in range(1, SSD_CONV):
            acc = acc + src[HALO - back:HALO - back + tm, :] * c[3 - back:4 - back]
        dst[:, off:off + tile] = (acc * _sigmoid(acc)).astype(BF16)

    up(0)
    for j in range(ntiles):
        if j + 1 < ntiles:
            up(j + 1)
        if j % 3 == 2:
            zc = slice((j // 3) * ztile, (j // 3 + 1) * ztile)
            z_ref[:, zc] = _dot(h, wz_ref[:, zc]).astype(z_ref.dtype)
        conv(j)
    dt_ref[...] = _dot(h, wdt_ref[...])
    dtt_ref[...] = _dot_nt(wdtt_ref[...], h)


def _ssd_core_kernel(x_ref, z_ref, xs_ref, bm_ref, cm_ref, dt_ref, dtt_ref, dtb_r, dtb_c, alog_r, alog_c,
                     dsk_ref, gnw_ref, rexp_ref, wo_ref, o_ref, st_ref, y_s, ex_s, we_s):
    tm, q = x_ref.shape[0], SSD_Q
    gw = SSD_HPG * SSD_P

    @pl.when(pl.program_id(1) == 0)
    def _():
        st_ref[...] = jnp.zeros_like(st_ref)

    dt = _softplus(dt_ref[...] + dtb_r[...])
    dtt = _softplus(dtt_ref[...] + dtb_c[...])
    a = dt * -jnp.exp(alog_r[...])
    at = dtt * -jnp.exp(alog_c[...])
    tri_l, tri_u = _tri(q, lower=True), _tri(q, lower=False)
    acs, acst, per_row = [], [], []
    for b in range(tm // q):
        rows = slice(b * q, (b + 1) * q)
        acs_b = _exact_left(tri_l, a[rows])
        acs.append(acs_b)
        acst.append(_exact_right(at[:, rows], tri_u))
        per_row.append(jnp.concatenate([jnp.exp(acs_b), dt[rows] * jnp.exp(acs_b[q - 1:q] - acs_b)], axis=1))
    ew = jnp.concatenate(per_row, axis=0)
    ew_hi = ew.astype(BF16)
    ew_parts = jnp.concatenate([ew_hi, (ew - ew_hi.astype(F32)).astype(BF16)], axis=1)
    ex_s[...] = _dot(ew_parts, rexp_ref[:, :SSD_DINNER])
    we_s[...] = (xs_ref[...].astype(F32) * _dot(ew_parts, rexp_ref[:, SSD_DINNER:])).astype(BF16)

    causal = (lax.broadcasted_iota(jnp.int32, (q, q), 0) >= lax.broadcasted_iota(jnp.int32, (q, q), 1))
    low_half = lax.broadcasted_iota(jnp.int32, (1, 2 * SSD_P), 1) < SSD_P
    state = [st_ref[g] for g in range(SSD_GROUPS)]
    for b in range(tm // q):
        rows = slice(b * q, (b + 1) * q)
        for g in range(SSD_GROUPS):
            ns = slice(g * SSD_N, (g + 1) * SSD_N)
            gs = slice(g * gw, (g + 1) * gw)
            bg = bm_ref[rows, ns]
            cg = cm_ref[rows, ns]
            cb = _dot_nt(cg, bg)
            update = _dot_tn(bg, we_s[rows, gs])
            pieces = []
            for pair in range(SSD_HPG // 2):
                cols = slice(g * gw + pair * 2 * SSD_P, g * gw + (pair + 1) * 2 * SSD_P)
                xp = xs_ref[rows, cols]
                ms = []
                for half in range(2):
                    hh = g * SSD_HPG + pair * 2 + half
                    seg = acs[b][:, hh:hh + 1] - acst[b][hh:hh + 1, :]
                    decay = jnp.exp(jnp.where(causal, seg, -1e30))
                    ms.append((cb * decay * dtt[hh:hh + 1, rows]).astype(BF16))
                xp2 = jnp.concatenate([jnp.where(low_half, xp, jnp.zeros_like(xp)),
                                       jnp.where(low_half, jnp.zeros_like(xp), xp)], axis=0)
                pieces.append(_dot(jnp.concatenate(ms, axis=1), xp2))
            y_s[rows, gs] = (_dot(cg, state[g].astype(BF16)) * ex_s[rows, gs]
                             + jnp.concatenate(pieces, axis=1))
            state[g] = state[g] * ex_s[(b + 1) * q - 1:(b + 1) * q, gs] + update
    for g in range(SSD_GROUPS):
        st_ref[g] = state[g]

    gnw = gnw_ref[...]
    dsk = dsk_ref[...]
    parts = []
    for g in range(SSD_GROUPS):
        gs = slice(g * gw, (g + 1) * gw)
        zz = z_ref[:, gs].astype(F32)
        y = (y_s[:, gs] + dsk[:, gs] * xs_ref[:, gs].astype(F32)) * (zz * _sigmoid(zz))
        parts.append((_rms(y) * gnw[:, gs]).astype(BF16))
    o_ref[...] = x_ref[...] + _dot(jnp.concatenate(parts, axis=-1), wo_ref[...])


def _ssd(x, nw, w_in, conv_w, conv_b, dt_bias, a_log, d_skip, norm_w, w_out, batch, seq_len, tm=512):
    t = x.shape[0]
    steps_per_seq = seq_len // tm
    conv_dim = SSD_DINNER + 2 * SSD_GN
    wb = w_in.astype(BF16)
    wz, wx, wdt = wb[:, :SSD_DINNER], wb[:, SSD_DINNER:SSD_DINNER + conv_dim], wb[:, SSD_DINNER + conv_dim:]
    cw = jnp.concatenate([conv_w, conv_b[None, :], jnp.zeros((3, conv_dim), F32)], axis=0)
    row = pl.BlockSpec((tm, D_MODEL), lambda i: (i, 0))

    def out_row(n):
        return pl.BlockSpec((tm, n), lambda i: (i, 0))

    z, xs, bm, cm, dt, dtt = pl.pallas_call(
        functools.partial(_ssd_proj_kernel, steps_per_seq=steps_per_seq),
        grid=(t // tm,),
        in_specs=[row, _prev_rows_spec(tm, steps_per_seq)] + [_vmem()] * 6,
        out_specs=[out_row(SSD_DINNER), out_row(SSD_DINNER), out_row(SSD_GN), out_row(SSD_GN),
                   out_row(SSD_HEADS), pl.BlockSpec((SSD_HEADS, tm), lambda i: (0, i))],
        out_shape=[jax.ShapeDtypeStruct((t, SSD_DINNER), BF16), jax.ShapeDtypeStruct((t, SSD_DINNER), BF16),
                   jax.ShapeDtypeStruct((t, SSD_GN), BF16), jax.ShapeDtypeStruct((t, SSD_GN), BF16),
                   jax.ShapeDtypeStruct((t, SSD_HEADS), F32), jax.ShapeDtypeStruct((SSD_HEADS, t), F32)],
        scratch_shapes=[pltpu.VMEM((tm + HALO, D_MODEL), BF16), pltpu.VMEM((tm + HALO, SSD_PROJ_TILE), F32),
                        pltpu.VMEM((tm + HALO, SSD_PROJ_TILE), F32)],
        compiler_params=_params("parallel"),
        name="ssd_proj",
    )(x, x, nw.reshape(1, D_MODEL), wz, wx, wdt, wdt.T, cw)

    spb = seq_len // tm
    spread = jnp.repeat(jnp.eye(SSD_HEADS, dtype=BF16), SSD_P, axis=1)
    zero = jnp.zeros_like(spread)
    top = jnp.concatenate([jnp.concatenate([spread, zero], axis=1), jnp.concatenate([zero, spread], axis=1)], axis=0)
    rexp = jnp.concatenate([top, top], axis=0)
    dsk = jnp.repeat(d_skip, SSD_P).reshape(1, SSD_DINNER)

    def blk(n):
        return pl.BlockSpec((tm, n), lambda b, i: (b * spb + i, 0))

    return pl.pallas_call(
        _ssd_core_kernel,
        grid=(batch, spb),
        in_specs=[blk(D_MODEL), blk(SSD_DINNER), blk(SSD_DINNER), blk(SSD_GN), blk(SSD_GN), blk(SSD_HEADS),
                  pl.BlockSpec((SSD_HEADS, tm), lambda b, i: (0, b * spb + i))] + [_vmem()] * 8,
        out_specs=blk(D_MODEL),
        out_shape=jax.ShapeDtypeStruct((t, D_MODEL), F32),
        scratch_shapes=[pltpu.VMEM((SSD_GROUPS, SSD_N, SSD_HPG * SSD_P), F32),
                        pltpu.VMEM((tm, SSD_DINNER), F32), pltpu.VMEM((tm, SSD_DINNER), F32),
                        pltpu.VMEM((tm, SSD_DINNER), BF16)],
        compiler_params=_params("parallel", "arbitrary"),
        name="ssd_core",
    )(x, z, xs, bm, cm, dt, dtt, dt_bias.reshape(1, SSD_HEADS), dt_bias.reshape(SSD_HEADS, 1),
      a_log.reshape(1, SSD_HEADS), a_log.reshape(SSD_HEADS, 1), dsk, norm_w.reshape(1, SSD_DINNER), rexp,
      w_out.astype(BF16))


def _sb_proj_kernel(x_ref, nw_ref, wq_ref, wk_ref, wv_ref, q_ref, k_ref, v_ref):
    h = (_rms(x_ref[...]) * nw_ref[...]).astype(BF16)
    q_ref[...] = (_dot(h, wq_ref[...]) * (SB_DH ** -0.5 * LOG2E)).astype(BF16)
    k_ref[...] = _dot(h, wk_ref[...]).astype(BF16)
    v_ref[...] = _dot(h, wv_ref[...]).astype(BF16)


def _sb_attn_kernel(q_ref, k_ref, v_ref, o_ref, acc_ref, z_s, a_s, kmax_s):
    t, nh, ns = SB_T, SB_HPS, SB_SLABS
    w = nh * SB_DH
    nc = ns * nh
    i = pl.program_id(2)
    head_of_lane = lax.broadcasted_iota(jnp.int32, (1, w), 1) // SB_DH

    def per_head(a):
        return jnp.concatenate([jnp.where(head_of_lane == h, a, jnp.zeros_like(a)) for h in range(nh)], axis=0)

    def key_rows(j):
        return pl.ds(pl.multiple_of(j * t, t), t)

    qs = [per_head(q_ref[0, :, sl * w:(sl + 1) * w]) for sl in range(ns)]
    upper = _tri(t, lower=True)
    strict = (lax.broadcasted_iota(jnp.int32, (t, t), 0) > lax.broadcasted_iota(jnp.int32, (t, t), 1))

    def logits(j, sl):
        return _dot_nt(qs[sl], k_ref[0, key_rows(j), sl * w:(sl + 1) * w])

    def survival(z_of, carry, masked, j_next):
        rcs, ys = [], []
        rows_next = key_rows(jnp.maximum(j_next, 0))
        k_next = [k_ref[0, rows_next, sl * w:(sl + 1) * w] for sl in range(ns)]
        for c in range(nc):
            z = z_of(c)
            sp = _softplus_base2(z)
            if masked:
                sp = jnp.where(strict, sp, 0.0)
            rcs.append(_dot(sp.astype(BF16), upper))
            ys.append(z - carry[c])
            sl, h = divmod(c, nh)
            z_s[c * t:(c + 1) * t, :] = _dot_nt(qs[sl][h * t:(h + 1) * t], k_next[sl])
        return rcs, ys

    def weights(rcs, ys, masked, chains):
        for c in chains:
            att = jnp.exp2(ys[c] - rcs[c])
            if masked:
                att = jnp.where(strict, att, 0.0)
            a_s[:, c * t:(c + 1) * t] = att.astype(BF16)

    def weighted_values(j):
        return [_dot(a_s[:, sl * nh * t:(sl + 1) * nh * t], per_head(v_ref[0, key_rows(j), sl * w:(sl + 1) * w]))
                for sl in range(ns)]

    zero = jnp.zeros((t, 1), F32)
    zz = [logits(i, sl) for sl in range(ns)]
    rcs, ys = survival(lambda c: zz[c // nh][(c % nh) * t:(c % nh + 1) * t], (zero,) * nc, True, i - 1)
    weights(rcs, ys, True, range(nc))
    acc_ref[...] = jnp.zeros_like(acc_ref)

    @pl.when(i == 0)
    def _():
        kmax_s[0] = jnp.max(jnp.abs(k_ref[0].astype(F32)))

    z_bound = SB_DH * jnp.max(jnp.abs(q_ref[0].astype(F32))) * kmax_s[0]
    dead = 150.0 + z_bound * (2.0 ** -7)

    def step(state):
        s, _, carry = state
        j = i - 1 - s
        pv = weighted_values(j + 1)
        rcs, ys = survival(lambda c: z_s[c * t:(c + 1) * t, :], carry, False, j - 1)
        for sl in range(ns):
            acc_ref[:, sl * w:(sl + 1) * w] += pv[sl]
        weights(rcs, ys, False, range(nc))
        carry = tuple(carry[c] + rcs[c][:, 0:1] for c in range(nc))
        lowest = carry[0]
        for c in range(1, nc):
            lowest = jnp.minimum(lowest, carry[c])
        return s + 1, jnp.min(lowest) < dead, carry

    done, _, _ = lax.while_loop(lambda state: jnp.logical_and(state[0] < i, state[1]), step,
                                (jnp.int32(0), jnp.bool_(True), tuple(rcs[c][:, 0:1] for c in range(nc))))
    pv = weighted_values(i - done)
    for sl in range(ns):
        o_ref[0, :, sl * w:(sl + 1) * w] = (acc_ref[:, sl * w:(sl + 1) * w] + pv[sl]).astype(o_ref.dtype)


def _out_proj_kernel(x_ref, a_ref, w_ref, o_ref):
    o_ref[...] = x_ref[...] + _dot(a_ref[...], w_ref[...])


def _sb(x, nw, w_qkv, w_out, batch, seq_len, tm=512):
    t = x.shape[0]
    wb = w_qkv.astype(BF16)
    row = pl.BlockSpec((tm, D_MODEL), lambda i: (i, 0))
    qkv_shape = jax.ShapeDtypeStruct((t, D_MODEL), BF16)
    q, k, v = pl.pallas_call(
        _sb_proj_kernel,
        grid=(t // tm,),
        in_specs=[row] + [_vmem()] * 4,
        out_specs=[row, row, row],
        out_shape=[qkv_shape, qkv_shape, qkv_shape],
        compiler_params=_params("parallel"),
        name="sb_proj",
    )(x, nw.reshape(1, D_MODEL), wb[:, :D_MODEL], wb[:, D_MODEL:2 * D_MODEL], wb[:, 2 * D_MODEL:])

    shape3 = (batch, seq_len, D_MODEL)
    lanes = SB_SLABS * SB_HPS * SB_DH
    chains = SB_SLABS * SB_HPS
    qblk = pl.BlockSpec((1, SB_T, lanes), lambda b, p, i: (b, i, p))
    kvblk = pl.BlockSpec((1, seq_len, lanes), lambda b, p, i: (b, 0, p))
    att = pl.pallas_call(
        _sb_attn_kernel,
        grid=(batch, SB_HEADS // chains, seq_len // SB_T),
        in_specs=[qblk, kvblk, kvblk],
        out_specs=qblk,
        out_shape=jax.ShapeDtypeStruct(shape3, BF16),
        scratch_shapes=[pltpu.VMEM((SB_T, lanes), F32), pltpu.VMEM((chains * SB_T, SB_T), F32),
                        pltpu.VMEM((SB_T, chains * SB_T), BF16), pltpu.SMEM((1,), F32)],
        compiler_params=_params("parallel", "parallel", "arbitrary"),
        name="sb_attention",
    )(q.reshape(shape3), k.reshape(shape3), v.reshape(shape3))

    return pl.pallas_call(
        _out_proj_kernel,
        grid=(t // tm,),
        in_specs=[row, row, _vmem()],
        out_specs=row,
        out_shape=jax.ShapeDtypeStruct((t, D_MODEL), F32),
        compiler_params=_params("parallel"),
        name="sb_out_proj",
    )(x, att.reshape(t, D_MODEL), w_out.astype(BF16))


def kernel(x, mix_norm_w, ffn_norm_w, final_norm_w, gla_w_in, gla_w_gate2, gla_b_gate, gla_norm_w, gla_w_out, pool_w, pool_b, pool_scale, ssd_w_in, ssd_conv_w, ssd_conv_b, ssd_dt_bias, ssd_a_log, ssd_d, ssd_norm_w, ssd_w_out, sb_w_qkv, sb_w_out, ffn_w_up, ffn_conv_w, ffn_conv_b, ffn_w_down):
    batch, seq_len, d = x.shape
    assert d == D_MODEL
    depth = mix_norm_w.shape[0]
    xf = x.reshape(batch * seq_len, d)
    for i in range(depth):
        m, j = i % 4, i // 4
        if m == 0:
            xf = _gla(xf, mix_norm_w[i], gla_w_in[j], gla_w_gate2[j], gla_b_gate[j], gla_norm_w[j], gla_w_out[j],
                      batch, seq_len)
        elif m == 1:
            xf = _pool(xf, mix_norm_w[i], pool_w[j], pool_b[j], pool_scale[j], seq_len)
        elif m == 2:
            xf = _ssd(xf, mix_norm_w[i], ssd_w_in[j], ssd_conv_w[j], ssd_conv_b[j], ssd_dt_bias[j], ssd_a_log[j],
                      ssd_d[j], ssd_norm_w[j], ssd_w_out[j], batch, seq_len)
        else:
            xf = _sb(xf, mix_norm_w[i], sb_w_qkv[j], sb_w_out[j], batch, seq_len)
        xf = _ffn(xf, ffn_norm_w[i], ffn_w_up[i], ffn_conv_w[i], ffn_conv_b[i], ffn_w_down[i], final_norm_w,
                  seq_len, final_norm=(i == depth - 1))
    return xf.reshape(batch, seq_len, d)
```

```python
import functools

import jax
import jax.numpy as jnp
from jax import lax
from jax.experimental import pallas as pl
from jax.experimental.pallas import tpu as pltpu

F32 = jnp.float32
BF16 = jnp.bfloat16

EPS = 1e-6
LOG2E = 1.4426950408889634
D_MODEL = 1024
HALO = 16
VMEM_LIMIT = 56 * 1024 * 1024

GLA_HEADS, GLA_DK, GLA_DV, GLA_RANK, GLA_TAU, GLA_CHUNK = 4, 128, 256, 16, 16.0, 64
GLA_HK, GLA_HV = GLA_HEADS * GLA_DK, GLA_HEADS * GLA_DV
POOL_WINDOWS, POOL_GW = (2, 4, 8, 16), 256
SSD_DINNER, SSD_P, SSD_HEADS, SSD_GROUPS, SSD_HPG, SSD_N, SSD_CONV = 2048, 64, 32, 4, 8, 128, 4
SSD_GN = SSD_GROUPS * SSD_N
SSD_Q = 128
SSD_PROJ_TILE = 256
SB_HEADS, SB_DH, SB_T = 16, 64, 256
SB_HPS = 4
SB_SLABS = 4
FFN_DIM, FFN_TILE, FFN_CONV = 2816, 256, 3
FFN_NT = FFN_DIM // FFN_TILE


def _params(*sem):
    return pltpu.CompilerParams(dimension_semantics=sem, vmem_limit_bytes=VMEM_LIMIT)


def _vmem():
    return pl.BlockSpec(memory_space=pltpu.VMEM)


def _rms(x):
    return x * lax.rsqrt(jnp.mean(x * x, axis=-1, keepdims=True) + EPS)


def _softplus(x):
    return jnp.maximum(x, 0.0) + jnp.log(1.0 + jnp.exp(-jnp.abs(x)))


def _softplus_base2(x):
    return jnp.where(x > 64.0, x, jnp.log(1.0 + jnp.exp2(x)) * LOG2E)


def _sigmoid(x):
    return 1.0 / (1.0 + jnp.exp(-x))


def _dot(a, b):
    return jnp.dot(a, b, preferred_element_type=F32)


def _dot_nt(a, b):
    return lax.dot_general(a, b, (((1,), (1,)), ((), ())), preferred_element_type=F32)


def _dot_tn(a, b):
    return lax.dot_general(a, b, (((0,), (0,)), ((), ())), preferred_element_type=F32)


def _split3(x):
    hi = x.astype(BF16)
    r = x - hi.astype(F32)
    mid = r.astype(BF16)
    lo = (r - mid.astype(F32)).astype(BF16)
    return hi, mid, lo


def _exact_left(m01, x):
    hi, mid, lo = _split3(x)
    return _dot(m01, lo) + _dot(m01, mid) + _dot(m01, hi)


def _exact_right(x, m01):
    hi, mid, lo = _split3(x)
    return _dot(lo, m01) + _dot(mid, m01) + _dot(hi, m01)


def _tri(n, lower):
    r = lax.broadcasted_iota(jnp.int32, (n, n), 0)
    c = lax.broadcasted_iota(jnp.int32, (n, n), 1)
    return jnp.where((r >= c) if lower else (r <= c), 1.0, 0.0).astype(BF16)


def _prev_rows_spec(tm, steps_per_seq):
    del steps_per_seq
    return pl.BlockSpec((HALO, D_MODEL), lambda i: (jnp.maximum(i * (tm // HALO) - 1, 0), 0))


def _normed_with_halo(x_ref, xp_ref, nw_ref, hs_ref, steps_per_seq):
    first = (pl.program_id(0) % steps_per_seq) == 0
    nw = nw_ref[...]
    hp = _rms(xp_ref[...]) * nw
    hs_ref[0:HALO, :] = jnp.where(first, 0.0, hp).astype(hs_ref.dtype)
    hs_ref[HALO:, :] = (_rms(x_ref[...]) * nw).astype(hs_ref.dtype)


def _ffn_kernel(x_ref, xp_ref, nw_ref, wu_ref, cw_ref, wd_ref, fw_ref, o_ref, hs_ref, a_ref,
                *, steps_per_seq, final_norm):
    _normed_with_halo(x_ref, xp_ref, nw_ref, hs_ref, steps_per_seq)

    def up(j):
        hs = hs_ref[...]
        lo = j * FFN_TILE
        return (_dot(hs, wu_ref[:, lo:lo + FFN_TILE]),
                _dot(hs, wu_ref[:, FFN_DIM + lo:FFN_DIM + lo + FFN_TILE]))

    def conv(u, lo):
        c = cw_ref[:, lo:lo + FFN_TILE]
        out = u[HALO:] * c[2:3] + c[3:4]
        out = out + pltpu.roll(u, 1, 0)[HALO:] * c[1:2]
        out = out + pltpu.roll(u, 2, 0)[HALO:] * c[0:1]
        return out

    u = up(0)
    for j in range(FFN_NT):
        nxt = up(j + 1) if j + 1 < FFN_NT else None
        g = conv(u[0], j * FFN_TILE)
        v = conv(u[1], FFN_DIM + j * FFN_TILE)
        a_ref[:, j * FFN_TILE:(j + 1) * FFN_TILE] = (g * _sigmoid(g) * v).astype(BF16)
        u = nxt
    y = x_ref[...] + _dot(a_ref[...], wd_ref[...])
    if final_norm:
        y = _rms(y) * fw_ref[...]
    o_ref[...] = y


def _ffn(x, nw, w_up, conv_w, conv_b, w_down, final_w, seq_len, final_norm, tm=1024):
    t = x.shape[0]
    steps_per_seq = seq_len // tm
    cw = jnp.concatenate([conv_w, conv_b[None, :], jnp.zeros((4, 2 * FFN_DIM), F32)], axis=0)
    row = pl.BlockSpec((tm, D_MODEL), lambda i: (i, 0))
    return pl.pallas_call(
        functools.partial(_ffn_kernel, steps_per_seq=steps_per_seq, final_norm=final_norm),
        grid=(t // tm,),
        in_specs=[row, _prev_rows_spec(tm, steps_per_seq)] + [_vmem()] * 5,
        out_specs=row,
        out_shape=jax.ShapeDtypeStruct((t, D_MODEL), F32),
        scratch_shapes=[pltpu.VMEM((tm + HALO, D_MODEL), BF16), pltpu.VMEM((tm, FFN_DIM), BF16)],
        compiler_params=_params("parallel"),
        name="conv_ffn",
    )(x, x, nw.reshape(1, D_MODEL), w_up.astype(BF16), cw, w_down.astype(BF16), final_w.reshape(1, D_MODEL))


def _gla_kernel(x_ref, nw_ref, wq_ref, wk_ref, wv_ref, wr_ref, wg1_ref, wg2_ref, bg_ref, gnw_ref, wo_ref,
                o_ref, st_ref, q_s, k_s, v_s, la_s, o_s):
    tm = x_ref.shape[0]

    @pl.when(pl.program_id(1) == 0)
    def _():
        st_ref[...] = jnp.zeros_like(st_ref)

    x = x_ref[...]
    h = (_rms(x) * nw_ref[...]).astype(BF16)
    q_s[...] = _dot(h, wq_ref[...]).astype(BF16)
    k_s[...] = _dot(h, wk_ref[...])
    v_s[...] = _dot(h, wv_ref[...]).astype(BF16)
    glr = _dot(h, wg1_ref[...]).astype(BF16)
    gate = _dot(glr, wg2_ref[...]) + bg_ref[...]
    la_s[...] = -_softplus(-gate) * (1.0 / GLA_TAU)

    tril = _tri(GLA_CHUNK, lower=True)
    scale = GLA_DK ** -0.5
    nchunks = tm // GLA_CHUNK
    k_cols = [slice(hd * GLA_DK, (hd + 1) * GLA_DK) for hd in range(GLA_HEADS)]
    v_cols = [slice(hd * GLA_DV, (hd + 1) * GLA_DV) for hd in range(GLA_HEADS)]

    def chunk_update(c):
        rows = slice(c * GLA_CHUNK, (c + 1) * GLA_CHUNK)
        g = _exact_left(tril, la_s[rows, :])
        g_end = g[GLA_CHUNK - 1:GLA_CHUNK, :]
        kd = (k_s[rows, :] * jnp.exp(g_end - g)).astype(BF16)
        vc = v_s[rows, :]
        return [_dot_tn(vc[:, v_cols[hd]], kd[:, k_cols[hd]]) for hd in range(GLA_HEADS)], jnp.exp(g_end)

    state = [st_ref[hd] for hd in range(GLA_HEADS)]
    pending = chunk_update(0)
    for c in range(nchunks):
        upd, eg = pending
        if c + 1 < nchunks:
            pending = chunk_update(c + 1)
        rows = slice(c * GLA_CHUNK, (c + 1) * GLA_CHUNK)
        qc = q_s[rows, :]
        for hd in range(GLA_HEADS):
            state[hd] = state[hd] * eg[:, k_cols[hd]] + upd[hd]
            o_s[rows, v_cols[hd]] = _dot_nt(qc[:, k_cols[hd]], state[hd].astype(BF16)) * scale
    for hd in range(GLA_HEADS):
        st_ref[hd] = state[hd]

    r = _dot(h, wr_ref[...])
    gated = r * _sigmoid(r)
    gnw = gnw_ref[...]
    parts = []
    for hd in range(GLA_HEADS):
        vs = slice(hd * GLA_DV, (hd + 1) * GLA_DV)
        parts.append((_rms(o_s[:, vs]) * gnw[:, vs] * gated[:, vs]).astype(BF16))
    o_ref[...] = x + _dot(jnp.concatenate(parts, axis=-1), wo_ref[...])


def _gla(x, nw, w_in, w_gate2, b_gate, norm_w, w_out, batch, seq_len, tm=1024):
    t = x.shape[0]
    spb = seq_len // tm
    wb = w_in.astype(BF16)
    wq, wk = wb[:, :GLA_HK], wb[:, GLA_HK:2 * GLA_HK]
    wv, wr = wb[:, 2 * GLA_HK:2 * GLA_HK + GLA_HV], wb[:, 2 * GLA_HK + GLA_HV:2 * GLA_HK + 2 * GLA_HV]
    wg1 = jnp.pad(wb[:, 2 * GLA_HK + 2 * GLA_HV:], ((0, 0), (0, 128 - GLA_RANK)))
    wg2 = jnp.pad(w_gate2.astype(BF16), ((0, 128 - GLA_RANK), (0, 0)))
    row = pl.BlockSpec((tm, D_MODEL), lambda b, i: (b * spb + i, 0))
    return pl.pallas_call(
        _gla_kernel,
        grid=(batch, spb),
        in_specs=[row] + [_vmem()] * 10,
        out_specs=row,
        out_shape=jax.ShapeDtypeStruct((t, D_MODEL), F32),
        scratch_shapes=[pltpu.VMEM((GLA_HEADS, GLA_DV, GLA_DK), F32),
                        pltpu.VMEM((tm, GLA_HK), BF16), pltpu.VMEM((tm, GLA_HK), F32),
                        pltpu.VMEM((tm, GLA_HV), BF16), pltpu.VMEM((tm, GLA_HK), F32),
                        pltpu.VMEM((tm, GLA_HV), F32)],
        compiler_params=_params("parallel", "arbitrary"),
        name="gla_mixer",
    )(x, nw.reshape(1, D_MODEL), wq, wk, wv, wr, wg1, wg2, b_gate.reshape(1, GLA_HK),
      norm_w.reshape(1, GLA_HV), w_out.astype(BF16))


def _pool_kernel(x_ref, xp_ref, nw_ref, w_ref, b_ref, sc_ref, o_ref, hs_ref, *, steps_per_seq):
    tm = x_ref.shape[0]
    _normed_with_halo(x_ref, xp_ref, nw_ref, hs_ref, steps_per_seq)
    pos = (pl.program_id(0) % steps_per_seq) * tm + lax.broadcasted_iota(jnp.int32, (tm, 1), 0)
    outs = []
    for gi, win in enumerate(POOL_WINDOWS):
        cols = slice(gi * POOL_GW, (gi + 1) * POOL_GW)
        cur = hs_ref[HALO:, cols]
        total = hs_ref[:, cols]
        span = 1
        while span < win:
            total = total + pltpu.roll(total, span, 0)
            span *= 2
        total = total[HALO:]
        count = jnp.minimum(pos + 1, win).astype(F32)
        dlt = (total / count - cur).astype(BF16)
        outs.append(_dot(dlt, w_ref[gi]) + b_ref[gi])
    o_ref[...] = x_ref[...] + jnp.concatenate(outs, axis=-1) * sc_ref[...]


def _pool(x, nw, w_grp, b_grp, scale, seq_len, tm=512):
    t = x.shape[0]
    steps_per_seq = seq_len // tm
    row = pl.BlockSpec((tm, D_MODEL), lambda i: (i, 0))
    return pl.pallas_call(
        functools.partial(_pool_kernel, steps_per_seq=steps_per_seq),
        grid=(t // tm,),
        in_specs=[row, _prev_rows_spec(tm, steps_per_seq), _vmem(), _vmem(), _vmem(), _vmem()],
        out_specs=row,
        out_shape=jax.ShapeDtypeStruct((t, D_MODEL), F32),
        scratch_shapes=[pltpu.VMEM((tm + HALO, D_MODEL), F32)],
        compiler_params=_params("parallel"),
        name="pool_mixer",
    )(x, x, nw.reshape(1, D_MODEL), w_grp.astype(BF16), b_grp.reshape(len(POOL_WINDOWS), 1, POOL_GW),
      scale.reshape(1, D_MODEL))


def _ssd_proj_kernel(x_ref, xp_ref, nw_ref, wz_ref, wx_ref, wdt_ref, wdtt_ref, cw_ref, z_ref, xs_ref, bm_ref,
                     cm_ref, dt_ref, dtt_ref, hs_ref, u0_ref, u1_ref, *, steps_per_seq):
    tm = x_ref.shape[0]
    _normed_with_halo(x_ref, xp_ref, nw_ref, hs_ref, steps_per_seq)
    h = hs_ref[HALO:, :]
    tile, ztile = SSD_PROJ_TILE, 2 * SSD_PROJ_TILE
    ntiles = (SSD_DINNER + 2 * SSD_GN) // tile
    bufs = (u0_ref, u1_ref)

    def up(j):
        bufs[j % 2][...] = _dot(hs_ref[...], wx_ref[:, j * tile:(j + 1) * tile])

    def conv(j):
        lo = j * tile
        c = cw_ref[:, lo:lo + tile]
        if lo < SSD_DINNER:
            dst, off = xs_ref, lo
        elif lo < SSD_DINNER + SSD_GN:
            dst, off = bm_ref, lo - SSD_DINNER
        else:
            dst, off = cm_ref, lo - SSD_DINNER - SSD_GN
        src = bufs[j % 2]
        acc = src[HALO:, :] * c[3:4] + c[4:5]
        for back in range(1, SSD_CONV):
            acc = acc + src[HALO - back:HALO - back + tm, :] * c[3 - back:4 - back]
        dst[:, off:off + tile] = (acc * _sigmoid(acc)).astype(BF16)

    up(0)
    for j in range(ntiles):
        if j + 1 < ntiles:
            up(j + 1)
        if j % 3 == 2:
            zc = slice((j // 3) * ztile, (j // 3 + 1) * ztile)
            z_ref[:, zc] = _dot(h, wz_ref[:, zc]).astype(z_ref.dtype)
        conv(j)
    dt_ref[...] = _dot(h, wdt_ref[...])
    dtt_ref[...] = _dot_nt(wdtt_ref[...], h)


def _ssd_core_kernel(x_ref, z_ref, xs_ref, bm_ref, cm_ref, dt_ref, dtt_ref, dtb_r, dtb_c, alog_r, alog_c,
                     dsk_ref, gnw_ref, rexp_ref, wo_ref, o_ref, st_ref, y_s, ex_s, we_s):
    tm, q = x_ref.shape[0], SSD_Q
    gw = SSD_HPG * SSD_P

    @pl.when(pl.program_id(1) == 0)
    def _():
        st_ref[...] = jnp.zeros_like(st_ref)

    dt = _softplus(dt_ref[...] + dtb_r[...])
    dtt = _softplus(dtt_ref[...] + dtb_c[...])
    a = dt * -jnp.exp(alog_r[...])
    at = dtt * -jnp.exp(alog_c[...])
    tri_l, tri_u = _tri(q, lower=True), _tri(q, lower=False)
    acs, acst, per_row = [], [], []
    for b in range(tm // q):
        rows = slice(b * q, (b + 1) * q)
        acs_b = _exact_left(tri_l, a[rows])
        acs.append(acs_b)
        acst.append(_exact_right(at[:, rows], tri_u))
        per_row.append(jnp.concatenate([jnp.exp(acs_b), dt[rows] * jnp.exp(acs_b[q - 1:q] - acs_b)], axis=1))
    ew = jnp.concatenate(per_row, axis=0)
    ew_hi = ew.astype(BF16)
    ew_parts = jnp.concatenate([ew_hi, (ew - ew_hi.astype(F32)).astype(BF16)], axis=1)
    ex_s[...] = _dot(ew_parts, rexp_ref[:, :SSD_DINNER])
    we_s[...] = (xs_ref[...].astype(F32) * _dot(ew_parts, rexp_ref[:, SSD_DINNER:])).astype(BF16)

    causal = (lax.broadcasted_iota(jnp.int32, (q, q), 0) >= lax.broadcasted_iota(jnp.int32, (q, q), 1))
    low_half = lax.broadcasted_iota(jnp.int32, (1, 2 * SSD_P), 1) < SSD_P
    state = [st_ref[g] for g in range(SSD_GROUPS)]
    for b in range(tm // q):
        rows = slice(b * q, (b + 1) * q)
        for g in range(SSD_GROUPS):
            ns = slice(g * SSD_N, (g + 1) * SSD_N)
            gs = slice(g * gw, (g + 1) * gw)
            bg = bm_ref[rows, ns]
            cg = cm_ref[rows, ns]
            cb = _dot_nt(cg, bg)
            update = _dot_tn(bg, we_s[rows, gs])
            pieces = []
            for pair in range(SSD_HPG // 2):
                cols = slice(g * gw + pair * 2 * SSD_P, g * gw + (pair + 1) * 2 * SSD_P)
                xp = xs_ref[rows, cols]
                ms = []
                for half in range(2):
                    hh = g * SSD_HPG + pair * 2 + half
                    seg = acs[b][:, hh:hh + 1] - acst[b][hh:hh + 1, :]
                    decay = jnp.exp(jnp.where(causal, seg, -1e30))
                    ms.append((cb * decay * dtt[hh:hh + 1, rows]).astype(BF16))
                xp2 = jnp.concatenate([jnp.where(low_half, xp, jnp.zeros_like(xp)),
                                       jnp.where(low_half, jnp.zeros_like(xp), xp)], axis=0)
                pieces.append(_dot(jnp.concatenate(ms, axis=1), xp2))
            y_s[rows, gs] = (_dot(cg, state[g].astype(BF16)) * ex_s[rows, gs]
                             + jnp.concatenate(pieces, axis=1))
            state[g] = state[g] * ex_s[(b + 1) * q - 1:(b + 1) * q, gs] + update
    for g in range(SSD_GROUPS):
        st_ref[g] = state[g]

    gnw = gnw_ref[...]
    dsk = dsk_ref[...]
    parts = []
    for g in range(SSD_GROUPS):
        gs = slice(g * gw, (g + 1) * gw)
        zz = z_ref[:, gs].astype(F32)
        y = (y_s[:, gs] + dsk[:, gs] * xs_ref[:, gs].astype(F32)) * (zz * _sigmoid(zz))
        parts.append((_rms(y) * gnw[:, gs]).astype(BF16))
    o_ref[...] = x_ref[...] + _dot(jnp.concatenate(parts, axis=-1), wo_ref[...])


def _ssd(x, nw, w_in, conv_w, conv_b, dt_bias, a_log, d_skip, norm_w, w_out, batch, seq_len, tm=512):
    t = x.shape[0]
    steps_per_seq = seq_len // tm
    conv_dim = SSD_DINNER + 2 * SSD_GN
    wb = w_in.astype(BF16)
    wz, wx, wdt = wb[:, :SSD_DINNER], wb[:, SSD_DINNER:SSD_DINNER + conv_dim], wb[:, SSD_DINNER + conv_dim:]
    cw = jnp.concatenate([conv_w, conv_b[None, :], jnp.zeros((3, conv_dim), F32)], axis=0)
    row = pl.BlockSpec((tm, D_MODEL), lambda i: (i, 0))

    def out_row(n):
        return pl.BlockSpec((tm, n), lambda i: (i, 0))

    z, xs, bm, cm, dt, dtt = pl.pallas_call(
        functools.partial(_ssd_proj_kernel, steps_per_seq=steps_per_seq),
        grid=(t // tm,),
        in_specs=[row, _prev_rows_spec(tm, steps_per_seq)] + [_vmem()] * 6,
        out_specs=[out_row(SSD_DINNER), out_row(SSD_DINNER), out_row(SSD_GN), out_row(SSD_GN),
                   out_row(SSD_HEADS), pl.BlockSpec((SSD_HEADS, tm), lambda i: (0, i))],
        out_shape=[jax.ShapeDtypeStruct((t, SSD_DINNER), BF16), jax.ShapeDtypeStruct((t, SSD_DINNER), BF16),
                   jax.ShapeDtypeStruct((t, SSD_GN), BF16), jax.ShapeDtypeStruct((t, SSD_GN), BF16),
                   jax.ShapeDtypeStruct((t, SSD_HEADS), F32), jax.ShapeDtypeStruct((SSD_HEADS, t), F32)],
        scratch_shapes=[pltpu.VMEM((tm + HALO, D_MODEL), BF16), pltpu.VMEM((tm + HALO, SSD_PROJ_TILE), F32),
                        pltpu.VMEM((tm + HALO, SSD_PROJ_TILE), F32)],
        compiler_params=_params("parallel"),
        name="ssd_proj",
    )(x, x, nw.reshape(1, D_MODEL), wz, wx, wdt, wdt.T, cw)

    spb = seq_len // tm
    spread = jnp.repeat(jnp.eye(SSD_HEADS, dtype=BF16), SSD_P, axis=1)
    zero = jnp.zeros_like(spread)
    top = jnp.concatenate([jnp.concatenate([spread, zero], axis=1), jnp.concatenate([zero, spread], axis=1)], axis=0)
    rexp = jnp.concatenate([top, top], axis=0)
    dsk = jnp.repeat(d_skip, SSD_P).reshape(1, SSD_DINNER)

    def blk(n):
        return pl.BlockSpec((tm, n), lambda b, i: (b * spb + i, 0))

    return pl.pallas_call(
        _ssd_core_kernel,
        grid=(batch, spb),
        in_specs=[blk(D_MODEL), blk(SSD_DINNER), blk(SSD_DINNER), blk(SSD_GN), blk(SSD_GN), blk(SSD_HEADS),
                  pl.BlockSpec((SSD_HEADS, tm), lambda b, i: (0, b * spb + i))] + [_vmem()] * 8,
        out_specs=blk(D_MODEL),
        out_shape=jax.ShapeDtypeStruct((t, D_MODEL), F32),
        scratch_shapes=[pltpu.VMEM((SSD_GROUPS, SSD_N, SSD_HPG * SSD_P), F32),
                        pltpu.VMEM((tm, SSD_DINNER), F32), pltpu.VMEM((tm, SSD_DINNER), F32),
                        pltpu.VMEM((tm, SSD_DINNER), BF16)],
        compiler_params=_params("parallel", "arbitrary"),
        name="ssd_core",
    )(x, z, xs, bm, cm, dt, dtt, dt_bias.reshape(1, SSD_HEADS), dt_bias.reshape(SSD_HEADS, 1),
      a_log.reshape(1, SSD_HEADS), a_log.reshape(SSD_HEADS, 1), dsk, norm_w.reshape(1, SSD_DINNER), rexp,
      w_out.astype(BF16))


def _sb_proj_kernel(x_ref, nw_ref, wq_ref, wk_ref, wv_ref, q_ref, k_ref, v_ref):
    h = (_rms(x_ref[...]) * nw_ref[...]).astype(BF16)
    q_ref[...] = (_dot(h, wq_ref[...]) * (SB_DH ** -0.5 * LOG2E)).astype(BF16)
    k_ref[...] = _dot(h, wk_ref[...]).astype(BF16)
    v_ref[...] = _dot(h, wv_ref[...]).astype(BF16)


def _sb_attn_kernel(q_ref, k_ref, v_ref, o_ref, acc_ref, z_s, a_s, kmax_s):
    t, nh, ns = SB_T, SB_HPS, SB_SLABS
    w = nh * SB_DH
    nc = ns * nh
    i = pl.program_id(2)
    head_of_lane = lax.broadcasted_iota(jnp.int32, (1, w), 1) // SB_DH

    def per_head(a):
        return jnp.concatenate([jnp.where(head_of_lane == h, a, jnp.zeros_like(a)) for h in range(nh)], axis=0)

    def key_rows(j):
        return pl.ds(pl.multiple_of(j * t, t), t)

    qs = [per_head(q_ref[0, :, sl * w:(sl + 1) * w]) for sl in range(ns)]
    upper = _tri(t, lower=True)
    strict = (lax.broadcasted_iota(jnp.int32, (t, t), 0) > lax.broadcasted_iota(jnp.int32, (t, t), 1))

    def logits(j, sl):
        return _dot_nt(qs[sl], k_ref[0, key_rows(j), sl * w:(sl + 1) * w])

    def survival(z_of, carry, masked, j_next):
        rcs, ys = [], []
        rows_next = key_rows(jnp.maximum(j_next, 0))
        k_next = [k_ref[0, rows_next, sl * w:(sl + 1) * w] for sl in range(ns)]
        for c in range(nc):
            z = z_of(c)
            sp = _softplus_base2(z)
            if masked:
                sp = jnp.where(strict, sp, 0.0)
            rcs.append(_dot(sp.astype(BF16), upper))
            ys.append(z - carry[c])
            sl, h = divmod(c, nh)
            z_s[c * t:(c + 1) * t, :] = _dot_nt(qs[sl][h * t:(h + 1) * t], k_next[sl])
        return rcs, ys

    def weights(rcs, ys, masked, chains):
        for c in chains:
            att = jnp.exp2(ys[c] - rcs[c])
            if masked:
                att = jnp.where(strict, att, 0.0)
            a_s[:, c * t:(c + 1) * t] = att.astype(BF16)

    def weighted_values(j):
        return [_dot(a_s[:, sl * nh * t:(sl + 1) * nh * t], per_head(v_ref[0, key_rows(j), sl * w:(sl + 1) * w]))
                for sl in range(ns)]

    zero = jnp.zeros((t, 1), F32)
    zz = [logits(i, sl) for sl in range(ns)]
    rcs, ys = survival(lambda c: zz[c // nh][(c % nh) * t:(c % nh + 1) * t], (zero,) * nc, True, i - 1)
    weights(rcs, ys, True, range(nc))
    acc_ref[...] = jnp.zeros_like(acc_ref)

    @pl.when(i == 0)
    def _():
        kmax_s[0] = jnp.max(jnp.abs(k_ref[0].astype(F32)))

    z_bound = SB_DH * jnp.max(jnp.abs(q_ref[0].astype(F32))) * kmax_s[0]
    dead = 150.0 + z_bound * (2.0 ** -7)

    def step(state):
        s, _, carry = state
        j = i - 1 - s
        pv = weighted_values(j + 1)
        rcs, ys = survival(lambda c: z_s[c * t:(c + 1) * t, :], carry, False, j - 1)
        for sl in range(ns):
            acc_ref[:, sl * w:(sl + 1) * w] += pv[sl]
        weights(rcs, ys, False, range(nc))
        carry = tuple(carry[c] + rcs[c][:, 0:1] for c in range(nc))
        lowest = carry[0]
        for c in range(1, nc):
            lowest = jnp.minimum(lowest, carry[c])
        return s + 1, jnp.min(lowest) < dead, carry

    done, _, _ = lax.while_loop(lambda state: jnp.logical_and(state[0] < i, state[1]), step,
                                (jnp.int32(0), jnp.bool_(True), tuple(rcs[c][:, 0:1] for c in range(nc))))
    pv = weighted_values(i - done)
    for sl in range(ns):
        o_ref[0, :, sl * w:(sl + 1) * w] = (acc_ref[:, sl * w:(sl + 1) * w] + pv[sl]).astype(o_ref.dtype)


def _out_proj_kernel(x_ref, a_ref, w_ref, o_ref):
    o_ref[...] = x_ref[...] + _dot(a_ref[...], w_ref[...])


def _sb(x, nw, w_qkv, w_out, batch, seq_len, tm=512):
    t = x.shape[0]
    wb = w_qkv.astype(BF16)
    row = pl.BlockSpec((tm, D_MODEL), lambda i: (i, 0))
    qkv_shape = jax.ShapeDtypeStruct((t, D_MODEL), BF16)
    q, k, v = pl.pallas_call(
        _sb_proj_kernel,
        grid=(t // tm,),
        in_specs=[row] + [_vmem()] * 4,
        out_specs=[row, row, row],
        out_shape=[qkv_shape, qkv_shape, qkv_shape],
        compiler_params=_params("parallel"),
        name="sb_proj",
    )(x, nw.reshape(1, D_MODEL), wb[:, :D_MODEL], wb[:, D_MODEL:2 * D_MODEL], wb[:, 2 * D_MODEL:])

    shape3 = (batch, seq_len, D_MODEL)
    lanes = SB_SLABS * SB_HPS * SB_DH
    chains = SB_SLABS * SB_HPS
    qblk = pl.BlockSpec((1, SB_T, lanes), lambda b, p, i: (b, i, p))
    kvblk = pl.BlockSpec((1, seq_len, lanes), lambda b, p, i: (b, 0, p))
    att = pl.pallas_call(
        _sb_attn_kernel,
        grid=(batch, SB_HEADS // chains, seq_len // SB_T),
        in_specs=[qblk, kvblk, kvblk],
        out_specs=qblk,
        out_shape=jax.ShapeDtypeStruct(shape3, BF16),
        scratch_shapes=[pltpu.VMEM((SB_T, lanes), F32), pltpu.VMEM((chains * SB_T, SB_T), F32),
                        pltpu.VMEM((SB_T, chains * SB_T), BF16), pltpu.SMEM((1,), F32)],
        compiler_params=_params("parallel", "parallel", "arbitrary"),
        name="sb_attention",
    )(q.reshape(shape3), k.reshape(shape3), v.reshape(shape3))

    return pl.pallas_call(
        _out_proj_kernel,
        grid=(t // tm,),
        in_specs=[row, row, _vmem()],
        out_specs=row,
        out_shape=jax.ShapeDtypeStruct((t, D_MODEL), F32),
        compiler_params=_params("parallel"),
        name="sb_out_proj",
    )(x, att.reshape(t, D_MODEL), w_out.astype(BF16))


def kernel(x, mix_norm_w, ffn_norm_w, final_norm_w, gla_w_in, gla_w_gate2, gla_b_gate, gla_norm_w, gla_w_out, pool_w, pool_b, pool_scale, ssd_w_in, ssd_conv_w, ssd_conv_b, ssd_dt_bias, ssd_a_log, ssd_d, ssd_norm_w, ssd_w_out, sb_w_qkv, sb_w_out, ffn_w_up, ffn_conv_w, ffn_conv_b, ffn_w_down):
    batch, seq_len, d = x.shape
    assert d == D_MODEL
    depth = mix_norm_w.shape[0]
    xf = x.reshape(batch * seq_len, d)
    for i in range(depth):
        m, j = i % 4, i // 4
        if m == 0:
            xf = _gla(xf, mix_norm_w[i], gla_w_in[j], gla_w_gate2[j], gla_b_gate[j], gla_norm_w[j], gla_w_out[j],
                      batch, seq_len)
        elif m == 1:
            xf = _pool(xf, mix_norm_w[i], pool_w[j], pool_b[j], pool_scale[j], seq_len)
        elif m == 2:
            xf = _ssd(xf, mix_norm_w[i], ssd_w_in[j], ssd_conv_w[j], ssd_conv_b[j], ssd_dt_bias[j], ssd_a_log[j],
                      ssd_d[j], ssd_norm_w[j], ssd_w_out[j], batch, seq_len)
        else:
            xf = _sb(xf, mix_norm_w[i], sb_w_qkv[j], sb_w_out[j], batch, seq_len)
        xf = _ffn(xf, ffn_norm_w[i], ffn_w_up[i], ffn_conv_w[i], ffn_conv_b[i], ffn_w_down[i], final_norm_w,
                  seq_len, final_norm=(i == depth - 1))
    return xf.reshape(batch, seq_len, d)
```

```python
import functools

import jax
import jax.numpy as jnp
from jax import lax
from jax.experimental import pallas as pl
from jax.experimental.pallas import tpu as pltpu

F32 = jnp.float32
BF16 = jnp.bfloat16

EPS = 1e-6
LOG2E = 1.4426950408889634
D_MODEL = 1024
HALO = 16
VMEM_LIMIT = 56 * 1024 * 1024

GLA_HEADS, GLA_DK, GLA_DV, GLA_RANK, GLA_TAU, GLA_CHUNK = 4, 128, 256, 16, 16.0, 64
GLA_HK, GLA_HV = GLA_HEADS * GLA_DK, GLA_HEADS * GLA_DV
POOL_WINDOWS, POOL_GW = (2, 4, 8, 16), 256
SSD_DINNER, SSD_P, SSD_HEADS, SSD_GROUPS, SSD_HPG, SSD_N, SSD_CONV = 2048, 64, 32, 4, 8, 128, 4
SSD_GN = SSD_GROUPS * SSD_N
SSD_Q = 128
SSD_PROJ_TILE = 256
SB_HEADS, SB_DH, SB_T = 16, 64, 256
SB_HPS = 4
SB_SLABS = 4
FFN_DIM, FFN_TILE, FFN_CONV = 2816, 256, 3
FFN_NT = FFN_DIM // FFN_TILE


def _params(*sem):
    return pltpu.CompilerParams(dimension_semantics=sem, vmem_limit_bytes=VMEM_LIMIT)


def _vmem():
    return pl.BlockSpec(memory_space=pltpu.VMEM)


def _rms(x):
    return x * lax.rsqrt(jnp.mean(x * x, axis=-1, keepdims=True) + EPS)


def _softplus(x):
    return jnp.maximum(x, 0.0) + jnp.log(1.0 + jnp.exp(-jnp.abs(x)))


def _softplus_base2(x):
    return jnp.where(x > 64.0, x, jnp.log(1.0 + jnp.exp2(x)) * LOG2E)


def _sigmoid(x):
    return 1.0 / (1.0 + jnp.exp(-x))


def _dot(a, b):
    return jnp.dot(a, b, preferred_element_type=F32)


def _dot_nt(a, b):
    return lax.dot_general(a, b, (((1,), (1,)), ((), ())), preferred_element_type=F32)


def _dot_tn(a, b):
    return lax.dot_general(a, b, (((0,), (0,)), ((), ())), preferred_element_type=F32)


def _split3(x):
    hi = x.astype(BF16)
    r = x - hi.astype(F32)
    mid = r.astype(BF16)
    lo = (r - mid.astype(F32)).astype(BF16)
    return hi, mid, lo


def _exact_left(m01, x):
    hi, mid, lo = _split3(x)
    return _dot(m01, lo) + _dot(m01, mid) + _dot(m01, hi)


def _exact_right(x, m01):
    hi, mid, lo = _split3(x)
    return _dot(lo, m01) + _dot(mid, m01) + _dot(hi, m01)


def _tri(n, lower):
    r = lax.broadcasted_iota(jnp.int32, (n, n), 0)
    c = lax.broadcasted_iota(jnp.int32, (n, n), 1)
    return jnp.where((r >= c) if lower else (r <= c), 1.0, 0.0).astype(BF16)


def _prev_rows_spec(tm, steps_per_seq):
    del steps_per_seq
    return pl.BlockSpec((HALO, D_MODEL), lambda i: (jnp.maximum(i * (tm // HALO) - 1, 0), 0))


def _normed_with_halo(x_ref, xp_ref, nw_ref, hs_ref, steps_per_seq):
    first = (pl.program_id(0) % steps_per_seq) == 0
    nw = nw_ref[...]
    hp = _rms(xp_ref[...]) * nw
    hs_ref[0:HALO, :] = jnp.where(first, 0.0, hp).astype(hs_ref.dtype)
    hs_ref[HALO:, :] = (_rms(x_ref[...]) * nw).astype(hs_ref.dtype)


def _ffn_kernel(x_ref, xp_ref, nw_ref, wu_ref, cw_ref, wd_ref, fw_ref, o_ref, hs_ref, a_ref,
                *, steps_per_seq, final_norm):
    _normed_with_halo(x_ref, xp_ref, nw_ref, hs_ref, steps_per_seq)

    def up(j):
        hs = hs_ref[...]
        lo = j * FFN_TILE
        return (_dot(hs, wu_ref[:, lo:lo + FFN_TILE]),
                _dot(hs, wu_ref[:, FFN_DIM + lo:FFN_DIM + lo + FFN_TILE]))

    def conv(u, lo):
        c = cw_ref[:, lo:lo + FFN_TILE]
        out = u[HALO:] * c[2:3] + c[3:4]
        out = out + pltpu.roll(u, 1, 0)[HALO:] * c[1:2]
        out = out + pltpu.roll(u, 2, 0)[HALO:] * c[0:1]
        return out

    u = up(0)
    for j in range(FFN_NT):
        nxt = up(j + 1) if j + 1 < FFN_NT else None
        g = conv(u[0], j * FFN_TILE)
        v = conv(u[1], FFN_DIM + j * FFN_TILE)
        a_ref[:, j * FFN_TILE:(j + 1) * FFN_TILE] = (g * _sigmoid(g) * v).astype(BF16)
        u = nxt
    y = x_ref[...] + _dot(a_ref[...], wd_ref[...])
    if final_norm:
        y = _rms(y) * fw_ref[...]
    o_ref[...] = y


def _ffn(x, nw, w_up, conv_w, conv_b, w_down, final_w, seq_len, final_norm, tm=1024):
    t = x.shape[0]
    steps_per_seq = seq_len // tm
    cw = jnp.concatenate([conv_w, conv_b[None, :], jnp.zeros((4, 2 * FFN_DIM), F32)], axis=0)
    row = pl.BlockSpec((tm, D_MODEL), lambda i: (i, 0))
    return pl.pallas_call(
        functools.partial(_ffn_kernel, steps_per_seq=steps_per_seq, final_norm=final_norm),
        grid=(t // tm,),
        in_specs=[row, _prev_rows_spec(tm, steps_per_seq)] + [_vmem()] * 5,
        out_specs=row,
        out_shape=jax.ShapeDtypeStruct((t, D_MODEL), F32),
        scratch_shapes=[pltpu.VMEM((tm + HALO, D_MODEL), BF16), pltpu.VMEM((tm, FFN_DIM), BF16)],
        compiler_params=_params("parallel"),
        name="conv_ffn",
    )(x, x, nw.reshape(1, D_MODEL), w_up.astype(BF16), cw, w_down.astype(BF16), final_w.reshape(1, D_MODEL))


def _gla_kernel(x_ref, nw_ref, wq_ref, wk_ref, wv_ref, wr_ref, wg1_ref, wg2_ref, bg_ref, gnw_ref, wo_ref,
                o_ref, st_ref, q_s, k_s, v_s, la_s, o_s):
    tm = x_ref.shape[0]

    @pl.when(pl.program_id(1) == 0)
    def _():
        st_ref[...] = jnp.zeros_like(st_ref)

    x = x_ref[...]
    h = (_rms(x) * nw_ref[...]).astype(BF16)
    q_s[...] = _dot(h, wq_ref[...]).astype(BF16)
    k_s[...] = _dot(h, wk_ref[...])
    v_s[...] = _dot(h, wv_ref[...]).astype(BF16)
    glr = _dot(h, wg1_ref[...]).astype(BF16)
    gate = _dot(glr, wg2_ref[...]) + bg_ref[...]
    la_s[...] = -_softplus(-gate) * (1.0 / GLA_TAU)

    tril = _tri(GLA_CHUNK, lower=True)
    scale = GLA_DK ** -0.5
    nchunks = tm // GLA_CHUNK
    k_cols = [slice(hd * GLA_DK, (hd + 1) * GLA_DK) for hd in range(GLA_HEADS)]
    v_cols = [slice(hd * GLA_DV, (hd + 1) * GLA_DV) for hd in range(GLA_HEADS)]

    def chunk_update(c):
        rows = slice(c * GLA_CHUNK, (c + 1) * GLA_CHUNK)
        g = _exact_left(tril, la_s[rows, :])
        g_end = g[GLA_CHUNK - 1:GLA_CHUNK, :]
        kd = (k_s[rows, :] * jnp.exp(g_end - g)).astype(BF16)
        vc = v_s[rows, :]
        return [_dot_tn(vc[:, v_cols[hd]], kd[:, k_cols[hd]]) for hd in range(GLA_HEADS)], jnp.exp(g_end)

    state = [st_ref[hd] for hd in range(GLA_HEADS)]
    pending = chunk_update(0)
    for c in range(nchunks):
        upd, eg = pending
        if c + 1 < nchunks:
            pending = chunk_update(c + 1)
        rows = slice(c * GLA_CHUNK, (c + 1) * GLA_CHUNK)
        qc = q_s[rows, :]
        for hd in range(GLA_HEADS):
            state[hd] = state[hd] * eg[:, k_cols[hd]] + upd[hd]
            o_s[rows, v_cols[hd]] = _dot_nt(qc[:, k_cols[hd]], state[hd].astype(BF16)) * scale
    for hd in range(GLA_HEADS):
        st_ref[hd] = state[hd]

    r = _dot(h, wr_ref[...])
    gated = r * _sigmoid(r)
    gnw = gnw_ref[...]
    parts = []
    for hd in range(GLA_HEADS):
        vs = slice(hd * GLA_DV, (hd + 1) * GLA_DV)
        parts.append((_rms(o_s[:, vs]) * gnw[:, vs] * gated[:, vs]).astype(BF16))
    o_ref[...] = x + _dot(jnp.concatenate(parts, axis=-1), wo_ref[...])


def _gla(x, nw, w_in, w_gate2, b_gate, norm_w, w_out, batch, seq_len, tm=1024):
    t = x.shape[0]
    spb = seq_len // tm
    wb = w_in.astype(BF16)
    wq, wk = wb[:, :GLA_HK], wb[:, GLA_HK:2 * GLA_HK]
    wv, wr = wb[:, 2 * GLA_HK:2 * GLA_HK + GLA_HV], wb[:, 2 * GLA_HK + GLA_HV:2 * GLA_HK + 2 * GLA_HV]
    wg1 = jnp.pad(wb[:, 2 * GLA_HK + 2 * GLA_HV:], ((0, 0), (0, 128 - GLA_RANK)))
    wg2 = jnp.pad(w_gate2.astype(BF16), ((0, 128 - GLA_RANK), (0, 0)))
    row = pl.BlockSpec((tm, D_MODEL), lambda b, i: (b * spb + i, 0))
    return pl.pallas_call(
        _gla_kernel,
        grid=(batch, spb),
        in_specs=[row] + [_vmem()] * 10,
        out_specs=row,
        out_shape=jax.ShapeDtypeStruct((t, D_MODEL), F32),
        scratch_shapes=[pltpu.VMEM((GLA_HEADS, GLA_DV, GLA_DK), F32),
                        pltpu.VMEM((tm, GLA_HK), BF16), pltpu.VMEM((tm, GLA_HK), F32),
                        pltpu.VMEM((tm, GLA_HV), BF16), pltpu.VMEM((tm, GLA_HK), F32),
                        pltpu.VMEM((tm, GLA_HV), F32)],
        compiler_params=_params("parallel", "arbitrary"),
        name="gla_mixer",
    )(x, nw.reshape(1, D_MODEL), wq, wk, wv, wr, wg1, wg2, b_gate.reshape(1, GLA_HK),
      norm_w.reshape(1, GLA_HV), w_out.astype(BF16))


def _pool_kernel(x_ref, xp_ref, nw_ref, w_ref, b_ref, sc_ref, o_ref, hs_ref, *, steps_per_seq):
    tm = x_ref.shape[0]
    _normed_with_halo(x_ref, xp_ref, nw_ref, hs_ref, steps_per_seq)
    pos = (pl.program_id(0) % steps_per_seq) * tm + lax.broadcasted_iota(jnp.int32, (tm, 1), 0)
    outs = []
    for gi, win in enumerate(POOL_WINDOWS):
        cols = slice(gi * POOL_GW, (gi + 1) * POOL_GW)
        cur = hs_ref[HALO:, cols]
        total = hs_ref[:, cols]
        span = 1
        while span < win:
            total = total + pltpu.roll(total, span, 0)
            span *= 2
        total = total[HALO:]
        count = jnp.minimum(pos + 1, win).astype(F32)
        dlt = (total / count - cur).astype(BF16)
        outs.append(_dot(dlt, w_ref[gi]) + b_ref[gi])
    o_ref[...] = x_ref[...] + jnp.concatenate(outs, axis=-1) * sc_ref[...]


def _pool(x, nw, w_grp, b_grp, scale, seq_len, tm=512):
    t = x.shape[0]
    steps_per_seq = seq_len // tm
    row = pl.BlockSpec((tm, D_MODEL), lambda i: (i, 0))
    return pl.pallas_call(
        functools.partial(_pool_kernel, steps_per_seq=steps_per_seq),
        grid=(t // tm,),
        in_specs=[row, _prev_rows_spec(tm, steps_per_seq), _vmem(), _vmem(), _vmem(), _vmem()],
        out_specs=row,
        out_shape=jax.ShapeDtypeStruct((t, D_MODEL), F32),
        scratch_shapes=[pltpu.VMEM((tm + HALO, D_MODEL), F32)],
        compiler_params=_params("parallel"),
        name="pool_mixer",
    )(x, x, nw.reshape(1, D_MODEL), w_grp.astype(BF16), b_grp.reshape(len(POOL_WINDOWS), 1, POOL_GW),
      scale.reshape(1, D_MODEL))


def _ssd_proj_kernel(x_ref, xp_ref, nw_ref, wz_ref, wx_ref, wdt_ref, wdtt_ref, cw_ref, z_ref, xs_ref, bm_ref,
                     cm_ref, dt_ref, dtt_ref, hs_ref, u0_ref, u1_ref, *, steps_per_seq):
    tm = x_ref.shape[0]
    _normed_with_halo(x_ref, xp_ref, nw_ref, hs_ref, steps_per_seq)
    h = hs_ref[HALO:, :]
    tile, ztile = SSD_PROJ_TILE, 2 * SSD_PROJ_TILE
    ntiles = (SSD_DINNER + 2 * SSD_GN) // tile
    bufs = (u0_ref, u1_ref)

    def up(j):
        bufs[j % 2][...] = _dot(hs_ref[...], wx_ref[:, j * tile:(j + 1) * tile])

    def conv(j):
        lo = j * tile
        c = cw_ref[:, lo:lo + tile]
        if lo < SSD_DINNER:
            dst, off = xs_ref, lo
        elif lo < SSD_DINNER + SSD_GN:
            dst, off = bm_ref, lo - SSD_DINNER
        else:
            dst, off = cm_ref, lo - SSD_DINNER - SSD_GN
        src = bufs[j % 2]
        acc = src[HALO:, :] * c[3:4] + c[4:5]
        for back in range(1, SSD_CONV):
            acc = acc + src[HALO - back:HALO - back + tm, :] * c[3 - back:4 - back]
        dst[:, off:off + tile] = (acc * _sigmoid(acc)).astype(BF16)

    up(0)
    for j in range(ntiles):
        if j + 1 < ntiles:
            up(j + 1)
        if j % 3 == 2:
            zc = slice((j // 3) * ztile, (j // 3 + 1) * ztile)
            z_ref[:, zc] = _dot(h, wz_ref[:, zc]).astype(z_ref.dtype)
        conv(j)
    dt_ref[...] = _dot(h, wdt_ref[...])
    dtt_ref[...] = _dot_nt(wdtt_ref[...], h)


def _ssd_core_kernel(x_ref, z_ref, xs_ref, bm_ref, cm_ref, dt_ref, dtt_ref, dtb_r, dtb_c, alog_r, alog_c,
                     dsk_ref, gnw_ref, rexp_ref, wo_ref, o_ref, st_ref, y_s, ex_s, we_s):
    tm, q = x_ref.shape[0], SSD_Q
    gw = SSD_HPG * SSD_P

    @pl.when(pl.program_id(1) == 0)
    def _():
        st_ref[...] = jnp.zeros_like(st_ref)

    dt = _softplus(dt_ref[...] + dtb_r[...])
    dtt = _softplus(dtt_ref[...] + dtb_c[...])
    a = dt * -jnp.exp(alog_r[...])
    at = dtt * -jnp.exp(alog_c[...])
    tri_l, tri_u = _tri(q, lower=True), _tri(q, lower=False)
    acs, acst, per_row = [], [], []
    for b in range(tm // q):
        rows = slice(b * q, (b + 1) * q)
        acs_b = _exact_left(tri_l, a[rows])
        acs.append(acs_b)
        acst.append(_exact_right(at[:, rows], tri_u))
        per_row.append(jnp.concatenate([jnp.exp(acs_b), dt[rows] * jnp.exp(acs_b[q - 1:q] - acs_b)], axis=1))
    ew = jnp.concatenate(per_row, axis=0)
    ew_hi = ew.astype(BF16)
    ew_parts = jnp.concatenate([ew_hi, (ew - ew_hi.astype(F32)).astype(BF16)], axis=1)
    ex_s[...] = _dot(ew_parts, rexp_ref[:, :SSD_DINNER])
    we_s[...] = (xs_ref[...].astype(F32) * _dot(ew_parts, rexp_ref[:, SSD_DINNER:])).astype(BF16)

    causal = (lax.broadcasted_iota(jnp.int32, (q, q), 0) >= lax.broadcasted_iota(jnp.int32, (q, q), 1))
    low_half = lax.broadcasted_iota(jnp.int32, (1, 2 * SSD_P), 1) < SSD_P
    state = [st_ref[g] for g in range(SSD_GROUPS)]
    for b in range(tm // q):
        rows = slice(b * q, (b + 1) * q)
        for g in range(SSD_GROUPS):
            ns = slice(g * SSD_N, (g + 1) * SSD_N)
            gs = slice(g * gw, (g + 1) * gw)
            bg = bm_ref[rows, ns]
            cg = cm_ref[rows, ns]
            cb = _dot_nt(cg, bg)
            update = _dot_tn(bg, we_s[rows, gs])
            pieces = []
            for pair in range(SSD_HPG // 2):
                cols = slice(g * gw + pair * 2 * SSD_P, g * gw + (pair + 1) * 2 * SSD_P)
                xp = xs_ref[rows, cols]
                ms = []
                for half in range(2):
                    hh = g * SSD_HPG + pair * 2 + half
                    seg = acs[b][:, hh:hh + 1] - acst[b][hh:hh + 1, :]
                    decay = jnp.exp(jnp.where(causal, seg, -1e30))
                    ms.append((cb * decay * dtt[hh:hh + 1, rows]).astype(BF16))
                xp2 = jnp.concatenate([jnp.where(low_half, xp, jnp.zeros_like(xp)),
                                       jnp.where(low_half, jnp.zeros_like(xp), xp)], axis=0)
                pieces.append(_dot(jnp.concatenate(ms, axis=1), xp2))
            y_s[rows, gs] = (_dot(cg, state[g].astype(BF16)) * ex_s[rows, gs]
                             + jnp.concatenate(pieces, axis=1))
            state[g] = state[g] * ex_s[(b + 1) * q - 1:(b + 1) * q, gs] + update
    for g in range(SSD_GROUPS):
        st_ref[g] = state[g]

    gnw = gnw_ref[...]
    dsk = dsk_ref[...]
    parts = []
    for g in range(SSD_GROUPS):
        gs = slice(g * gw, (g + 1) * gw)
        zz = z_ref[:, gs].astype(F32)
        y = (y_s[:, gs] + dsk[:, gs] * xs_ref[:, gs].astype(F32)) * (zz * _sigmoid(zz))
        parts.append((_rms(y) * gnw[:, gs]).astype(BF16))
    o_ref[...] = x_ref[...] + _dot(jnp.concatenate(parts, axis=-1), wo_ref[...])


def _ssd(x, nw, w_in, conv_w, conv_b, dt_bias, a_log, d_skip, norm_w, w_out, batch, seq_len, tm=512):
    t = x.shape[0]
    steps_per_seq = seq_len // tm
    conv_dim = SSD_DINNER + 2 * SSD_GN
    wb = w_in.astype(BF16)
    wz, wx, wdt = wb[:, :SSD_DINNER], wb[:, SSD_DINNER:SSD_DINNER + conv_dim], wb[:, SSD_DINNER + conv_dim:]
    cw = jnp.concatenate([conv_w, conv_b[None, :], jnp.zeros((3, conv_dim), F32)], axis=0)
    row = pl.BlockSpec((tm, D_MODEL), lambda i: (i, 0))

    def out_row(n):
        return pl.BlockSpec((tm, n), lambda i: (i, 0))

    z, xs, bm, cm, dt, dtt = pl.pallas_call(
        functools.partial(_ssd_proj_kernel, steps_per_seq=steps_per_seq),
        grid=(t // tm,),
        in_specs=[row, _prev_rows_spec(tm, steps_per_seq)] + [_vmem()] * 6,
        out_specs=[out_row(SSD_DINNER), out_row(SSD_DINNER), out_row(SSD_GN), out_row(SSD_GN),
                   out_row(SSD_HEADS), pl.BlockSpec((SSD_HEADS, tm), lambda i: (0, i))],
        out_shape=[jax.ShapeDtypeStruct((t, SSD_DINNER), BF16), jax.ShapeDtypeStruct((t, SSD_DINNER), BF16),
                   jax.ShapeDtypeStruct((t, SSD_GN), BF16), jax.ShapeDtypeStruct((t, SSD_GN), BF16),
                   jax.ShapeDtypeStruct((t, SSD_HEADS), F32), jax.ShapeDtypeStruct((SSD_HEADS, t), F32)],
        scratch_shapes=[pltpu.VMEM((tm + HALO, D_MODEL), BF16), pltpu.VMEM((tm + HALO, SSD_PROJ_TILE), F32),
                        pltpu.VMEM((tm + HALO, SSD_PROJ_TILE), F32)],
        compiler_params=_params("parallel"),
        name="ssd_proj",
    )(x, x, nw.reshape(1, D_MODEL), wz, wx, wdt, wdt.T, cw)

    spb = seq_len // tm
    spread = jnp.repeat(jnp.eye(SSD_HEADS, dtype=BF16), SSD_P, axis=1)
    zero = jnp.zeros_like(spread)
    top = jnp.concatenate([jnp.concatenate([spread, zero], axis=1), jnp.concatenate([zero, spread], axis=1)], axis=0)
    rexp = jnp.concatenate([top, top], axis=0)
    dsk = jnp.repeat(d_skip, SSD_P).reshape(1, SSD_DINNER)

    def blk(n):
        return pl.BlockSpec((tm, n), lambda b, i: (b * spb + i, 0))

    return pl.pallas_call(
        _ssd_core_kernel,
        grid=(batch, spb),
        in_specs=[blk(D_MODEL), blk(SSD_DINNER), blk(SSD_DINNER), blk(SSD_GN), blk(SSD_GN), blk(SSD_HEADS),
                  pl.BlockSpec((SSD_HEADS, tm), lambda b, i: (0, b * spb + i))] + [_vmem()] * 8,
        out_specs=blk(D_MODEL),
        out_shape=jax.ShapeDtypeStruct((t, D_MODEL), F32),
        scratch_shapes=[pltpu.VMEM((SSD_GROUPS, SSD_N, SSD_HPG * SSD_P), F32),
                        pltpu.VMEM((tm, SSD_DINNER), F32), pltpu.VMEM((tm, SSD_DINNER), F32),
                        pltpu.VMEM((tm, SSD_DINNER), BF16)],
        compiler_params=_params("parallel", "arbitrary"),
        name="ssd_core",
    )(x, z, xs, bm, cm, dt, dtt, dt_bias.reshape(1, SSD_HEADS), dt_bias.reshape(SSD_HEADS, 1),
      a_log.reshape(1, SSD_HEADS), a_log.reshape(SSD_HEADS, 1), dsk, norm_w.reshape(1, SSD_DINNER), rexp,
      w_out.astype(BF16))


def _sb_proj_kernel(x_ref, nw_ref, wq_ref, wk_ref, wv_ref, q_ref, k_ref, v_ref):
    h = (_rms(x_ref[...]) * nw_ref[...]).astype(BF16)
    q_ref[...] = (_dot(h, wq_ref[...]) * (SB_DH ** -0.5 * LOG2E)).astype(BF16)
    k_ref[...] = _dot(h, wk_ref[...]).astype(BF16)
    v_ref[...] = _dot(h, wv_ref[...]).astype(BF16)


def _sb_attn_kernel(q_ref, k_ref, v_ref, x_ref, wo_ref, o_ref, acc_ref, z_s, a_s, kmax_s):
    t, nh, ns = SB_T, SB_HPS, SB_SLABS
    w = nh * SB_DH
    nc = ns * nh
    i = pl.program_id(2)
    head_of_lane = lax.broadcasted_iota(jnp.int32, (1, w), 1) // SB_DH

    def per_head(a):
        return jnp.concatenate([jnp.where(head_of_lane == h, a, jnp.zeros_like(a)) for h in range(nh)], axis=0)

    def key_rows(j):
        return pl.ds(pl.multiple_of(j * t, t), t)

    qs = [per_head(q_ref[0, :, sl * w:(sl + 1) * w]) for sl in range(ns)]
    upper = _tri(t, lower=True)
    strict = (lax.broadcasted_iota(jnp.int32, (t, t), 0) > lax.broadcasted_iota(jnp.int32, (t, t), 1))

    def logits(j, sl):
        return _dot_nt(qs[sl], k_ref[0, key_rows(j), sl * w:(sl + 1) * w])

    def survival(z_of, carry, masked, j_next):
        rcs, ys = [], []
        rows_next = key_rows(jnp.maximum(j_next, 0))
        k_next = [k_ref[0, rows_next, sl * w:(sl + 1) * w] for sl in range(ns)]
        for c in range(nc):
            z = z_of(c)
            sp = _softplus_base2(z)
            if masked:
                sp = jnp.where(strict, sp, 0.0)
            rcs.append(_dot(sp.astype(BF16), upper))
            ys.append(z - carry[c])
            sl, h = divmod(c, nh)
            z_s[c * t:(c + 1) * t, :] = _dot_nt(qs[sl][h * t:(h + 1) * t], k_next[sl])
        return rcs, ys

    def weights(rcs, ys, masked, chains):
        for c in chains:
            att = jnp.exp2(ys[c] - rcs[c])
            if masked:
                att = jnp.where(strict, att, 0.0)
            a_s[:, c * t:(c + 1) * t] = att.astype(BF16)

    def weighted_values(j):
        return [_dot(a_s[:, sl * nh * t:(sl + 1) * nh * t], per_head(v_ref[0, key_rows(j), sl * w:(sl + 1) * w]))
                for sl in range(ns)]

    zero = jnp.zeros((t, 1), F32)
    zz = [logits(i, sl) for sl in range(ns)]
    rcs, ys = survival(lambda c: zz[c // nh][(c % nh) * t:(c % nh + 1) * t], (zero,) * nc, True, i - 1)
    weights(rcs, ys, True, range(nc))
    acc_ref[...] = jnp.zeros_like(acc_ref)

    @pl.when(i == 0)
    def _():
        kmax_s[0] = jnp.max(jnp.abs(k_ref[0].astype(F32)))

    z_bound = SB_DH * jnp.max(jnp.abs(q_ref[0].astype(F32))) * kmax_s[0]
    dead = 150.0 + z_bound * (2.0 ** -7)

    def step(state):
        s, _, carry = state
        j = i - 1 - s
        pv = weighted_values(j + 1)
        rcs, ys = survival(lambda c: z_s[c * t:(c + 1) * t, :], carry, False, j - 1)
        for sl in range(ns):
            acc_ref[:, sl * w:(sl + 1) * w] += pv[sl]
        weights(rcs, ys, False, range(nc))
        carry = tuple(carry[c] + rcs[c][:, 0:1] for c in range(nc))
        lowest = carry[0]
        for c in range(1, nc):
            lowest = jnp.minimum(lowest, carry[c])
        return s + 1, jnp.min(lowest) < dead, carry

    done, _, _ = lax.while_loop(lambda state: jnp.logical_and(state[0] < i, state[1]), step,
                                (jnp.int32(0), jnp.bool_(True), tuple(rcs[c][:, 0:1] for c in range(nc))))
    pv = weighted_values(i - done)
    mixed = jnp.concatenate([(acc_ref[:, sl * w:(sl + 1) * w] + pv[sl]).astype(BF16) for sl in range(ns)], axis=1)
    o_ref[0] = x_ref[0] + _dot(mixed, wo_ref[...])


def _sb(x, nw, w_qkv, w_out, batch, seq_len, tm=512):
    t = x.shape[0]
    wb = w_qkv.astype(BF16)
    row = pl.BlockSpec((tm, D_MODEL), lambda i: (i, 0))
    qkv_shape = jax.ShapeDtypeStruct((t, D_MODEL), BF16)
    q, k, v = pl.pallas_call(
        _sb_proj_kernel,
        grid=(t // tm,),
        in_specs=[row] + [_vmem()] * 4,
        out_specs=[row, row, row],
        out_shape=[qkv_shape, qkv_shape, qkv_shape],
        compiler_params=_params("parallel"),
        name="sb_proj",
    )(x, nw.reshape(1, D_MODEL), wb[:, :D_MODEL], wb[:, D_MODEL:2 * D_MODEL], wb[:, 2 * D_MODEL:])

    shape3 = (batch, seq_len, D_MODEL)
    lanes = SB_SLABS * SB_HPS * SB_DH
    chains = SB_SLABS * SB_HPS
    qblk = pl.BlockSpec((1, SB_T, lanes), lambda b, p, i: (b, i, p))
    kvblk = pl.BlockSpec((1, seq_len, lanes), lambda b, p, i: (b, 0, p), pipeline_mode=pl.Buffered(1))
    assert chains == SB_HEADS
    xblk = pl.BlockSpec((1, SB_T, D_MODEL), lambda b, p, i: (b, i, 0))
    out = pl.pallas_call(
        _sb_attn_kernel,
        grid=(batch, SB_HEADS // chains, seq_len // SB_T),
        in_specs=[qblk, kvblk, kvblk, xblk, _vmem()],
        out_specs=xblk,
        out_shape=jax.ShapeDtypeStruct(shape3, F32),
        scratch_shapes=[pltpu.VMEM((SB_T, lanes), F32), pltpu.VMEM((chains * SB_T, SB_T), F32),
                        pltpu.VMEM((SB_T, chains * SB_T), BF16), pltpu.SMEM((1,), F32)],
        compiler_params=_params("parallel", "parallel", "arbitrary"),
        name="sb_attention",
    )(q.reshape(shape3), k.reshape(shape3), v.reshape(shape3), x.reshape(shape3), w_out.astype(BF16))
    return out.reshape(t, D_MODEL)


def kernel(x, mix_norm_w, ffn_norm_w, final_norm_w, gla_w_in, gla_w_gate2, gla_b_gate, gla_norm_w, gla_w_out, pool_w, pool_b, pool_scale, ssd_w_in, ssd_conv_w, ssd_conv_b, ssd_dt_bias, ssd_a_log, ssd_d, ssd_norm_w, ssd_w_out, sb_w_qkv, sb_w_out, ffn_w_up, ffn_conv_w, ffn_conv_b, ffn_w_down):
    batch, seq_len, d = x.shape
    assert d == D_MODEL
    depth = mix_norm_w.shape[0]
    xf = x.reshape(batch * seq_len, d)
    for i in range(depth):
        m, j = i % 4, i // 4
        if m == 0:
            xf = _gla(xf, mix_norm_w[i], gla_w_in[j], gla_w_gate2[j], gla_b_gate[j], gla_norm_w[j], gla_w_out[j],
                      batch, seq_len)
        elif m == 1:
            xf = _pool(xf, mix_norm_w[i], pool_w[j], pool_b[j], pool_scale[j], seq_len)
        elif m == 2:
            xf = _ssd(xf, mix_norm_w[i], ssd_w_in[j], ssd_conv_w[j], ssd_conv_b[j], ssd_dt_bias[j], ssd_a_log[j],
                      ssd_d[j], ssd_norm_w[j], ssd_w_out[j], batch, seq_len)
        else:
            xf = _sb(xf, mix_norm_w[i], sb_w_qkv[j], sb_w_out[j], batch, seq_len)
        xf = _ffn(xf, ffn_norm_w[i], ffn_w_up[i], ffn_conv_w[i], ffn_conv_b[i], ffn_w_down[i], final_norm_w,
                  seq_len, final_norm=(i == depth - 1))
    return xf.reshape(batch, seq_len, d)
```
